```python
import math
import jax, jax.numpy as jnp
from jax import lax
import numpy as np

D_MODEL = 1024
BATCH = 8
SEQ = 4096
DEPTH = 2

SSM_GROUP_CH = 16
SSM_GROUPS = D_MODEL // 32
SSM_WIDTH = SSM_GROUPS * SSM_GROUP_CH
SSM_STATE = 64
DT_MIN = 1e-3
DT_MAX = 1e-1
EIG_CLIP = 1e-4
HEAD_DIM = 64
ATTN_HEADS = D_MODEL // 128
ATTN_WIDTH = ATTN_HEADS * HEAD_DIM
Q_BLOCK = 128
N_IN = SSM_WIDTH + 3 * ATTN_WIDTH + ATTN_HEADS + 2 * D_MODEL
D_FF = ((8 * D_MODEL + 3 * 256 - 1) // (3 * 256)) * 256
N_MOD = 6
RMS_EPS = 1e-6

kernel_name = "hybrid_s5_fox_gated_block"


def rmsnorm(x, g):
    xf = x.astype(jnp.float32)
    r = lax.rsqrt(jnp.mean(xf * xf, axis=-1, keepdims=True) + RMS_EPS)
    return (xf * r * g.astype(jnp.float32)).astype(x.dtype)


def _linear_recurrence(e1, e2):
    a1, b1 = e1
    a2, b2 = e2
    return a1 * a2, a2 * b1 + b2


def s5_branch(u, lam_re, lam_im, log_dt, b_re, b_im, c_re, c_im, d_skip, w_glu, b_glu):
    dtype = u.dtype
    bsz, s, _ = u.shape
    f32 = jnp.float32
    uf = u.astype(f32).reshape(bsz, s, SSM_GROUPS, SSM_GROUP_CH)
    lam = lax.complex(jnp.minimum(lam_re.astype(f32), -EIG_CLIP), lam_im.astype(f32))
    dt = jnp.exp(log_dt.astype(f32))[:, None]
    lam_bar = jnp.exp(lam * dt)
    b = lax.complex(b_re.astype(f32), b_im.astype(f32))
    b_bar = ((lam_bar - 1.0) / lam)[..., None] * b
    bu = jnp.einsum('bsgh,gph->bsgp', uf, b_bar)
    a = jnp.broadcast_to(lam_bar, bu.shape)
    _, states = lax.associative_scan(_linear_recurrence, (a, bu), axis=1)
    cm = lax.complex(c_re.astype(f32), c_im.astype(f32))
    y = jnp.real(jnp.einsum('bsgp,ghp->bsgh', states, cm))
    y = y + d_skip.astype(f32).reshape(SSM_GROUPS, SSM_GROUP_CH) * uf
    y = y.reshape(bsz, s, SSM_WIDTH).astype(dtype)
    z = jax.nn.gelu(y)
    return z * jax.nn.sigmoid(z @ w_glu + b_glu)


def forgetting_attention(q, k, v, f_logit, b_f):
    bsz, s, _ = q.shape
    nb = s // Q_BLOCK
    f32 = jnp.float32
    q = q.reshape(bsz, s, ATTN_HEADS, HEAD_DIM).transpose(0, 2, 1, 3)
    k = k.reshape(bsz, s, ATTN_HEADS, HEAD_DIM).transpose(0, 2, 1, 3)
    v = v.reshape(bsz, s, ATTN_HEADS, HEAD_DIM).transpose(0, 2, 1, 3)
    log_f = jax.nn.log_sigmoid(f_logit.astype(f32) + b_f.astype(f32))
    cum = jnp.cumsum(log_f, axis=1).transpose(0, 2, 1)
    q_blocks = q.reshape(bsz, ATTN_HEADS, nb, Q_BLOCK, HEAD_DIM).transpose(2, 0, 1, 3, 4)
    cum_blocks = cum.reshape(bsz, ATTN_HEADS, nb, Q_BLOCK).transpose(2, 0, 1, 3)
    k_pos = jnp.arange(s)
    scale = HEAD_DIM ** -0.5

    def one_block(args):
        qb, cb, i = args
        logits = jnp.einsum('bhqd,bhkd->bhqk', qb, k).astype(f32) * scale
        logits = logits + cb[..., None] - cum[:, :, None, :]
        q_pos = i * Q_BLOCK + jnp.arange(Q_BLOCK)
        logits = jnp.where(k_pos[None, :] <= q_pos[:, None], logits, -jnp.inf)
        p = jax.nn.softmax(logits, axis=-1).astype(v.dtype)
        return jnp.einsum('bhqk,bhkd->bhqd', p, v)

    out = lax.map(one_block, (q_blocks, cum_blocks, jnp.arange(nb)))
    return out.transpose(1, 0, 3, 2, 4).reshape(bsz, s, ATTN_WIDTH)


def setup_inputs(seed: int = 0) -> dict:
    key = jax.random.key(seed)
    ks = jax.random.split(key, 32)
    f32 = jnp.float32
    nrm = lambda k, shape, s: jax.random.normal(k, shape, f32) * s
    L, D, G, P, H = DEPTH, D_MODEL, SSM_GROUPS, SSM_STATE, SSM_GROUP_CH
    lam_im0 = jnp.pi * jnp.arange(P, dtype=f32)
    return {
        "x": nrm(ks[0], (BATCH, SEQ, D), 1.0),
        "c": nrm(ks[1], (BATCH, D), 1.0),
        "w_ada": nrm(ks[2], (L, D, N_MOD * D), 0.5 * D ** -0.5),
        "b_ada": nrm(ks[3], (L, N_MOD * D), 0.02),
        "g_pre_mix": 1.0 + nrm(ks[4], (L, D), 0.02),
        "g_post_mix": 1.0 + nrm(ks[5], (L, D), 0.02),
        "g_pre_ffn": 1.0 + nrm(ks[6], (L, D), 0.02),
        "g_post_ffn": 1.0 + nrm(ks[7], (L, D), 0.02),
        "w_in": nrm(ks[8], (L, D, N_IN), D ** -0.5),
        "lam_re": -0.5 + nrm(ks[9], (L, G, P), 0.01),
        "lam_im": lam_im0 + nrm(ks[10], (L, G, P), 0.01),
        "log_dt": jax.random.uniform(ks[11], (L, G), f32, math.log(DT_MIN), math.log(DT_MAX)),
        "b_re": nrm(ks[12], (L, G, P, H), (2 * H) ** -0.5),
        "b_im": nrm(ks[13], (L, G, P, H), (2 * H) ** -0.5),
        "c_re": nrm(ks[14], (L, G, H, P), (2 * P) ** -0.5),
        "c_im": nrm(ks[15], (L, G, H, P), (2 * P) ** -0.5),
        "d_skip": nrm(ks[16], (L, SSM_WIDTH), 1.0),
        "w_glu": nrm(ks[17], (L, SSM_WIDTH, SSM_WIDTH), SSM_WIDTH ** -0.5),
        "b_glu": nrm(ks[18], (L, SSM_WIDTH), 0.02),
        "b_f": jax.random.uniform(ks[19], (L, ATTN_HEADS), f32, 1.0, 5.0),
        "w_pa": nrm(ks[20], (L, SSM_WIDTH, D), SSM_WIDTH ** -0.5),
        "w_pb": nrm(ks[21], (L, ATTN_WIDTH, D), ATTN_WIDTH ** -0.5),
        "w_o": nrm(ks[22], (L, D, D), D ** -0.5),
        "w_ffn_gate": nrm(ks[23], (L, D, D_FF), D ** -0.5),
        "w_ffn_up": nrm(ks[24], (L, D, D_FF), D ** -0.5),
        "w_ffn_down": nrm(ks[25], (L, D_FF, D), D_FF ** -0.5),
    }


def reference(x, c, w_ada, b_ada, g_pre_mix, g_post_mix, g_pre_ffn, g_post_ffn, w_in,
              lam_re, lam_im, log_dt, b_re, b_im, c_re, c_im, d_skip, w_glu, b_glu, b_f,
              w_pa, w_pb, w_o, w_ffn_gate, w_ffn_up, w_ffn_down):
    split_at = np.cumsum([SSM_WIDTH, ATTN_WIDTH, ATTN_WIDTH, ATTN_WIDTH, ATTN_HEADS, D_MODEL]).tolist()
    cond = jax.nn.silu(c)
    for l in range(DEPTH):
        mod = cond @ w_ada[l] + b_ada[l]
        shift_m, scale_m, gate_m, shift_f, scale_f, gate_f = jnp.split(mod[:, None, :], N_MOD, axis=-1)

        h = rmsnorm(x, g_pre_mix[l]) * (1.0 + scale_m) + shift_m
        proj = h @ w_in[l]
        u_ssm, q, k, v, f_logit, g_a, g_b = jnp.split(proj, split_at, axis=-1)
        y_ssm = s5_branch(u_ssm, lam_re[l], lam_im[l], log_dt[l], b_re[l], b_im[l],
                          c_re[l], c_im[l], d_skip[l], w_glu[l], b_glu[l])
        y_att = forgetting_attention(q, k, v, f_logit, b_f[l])
        merged = jax.nn.sigmoid(g_a) * (y_ssm @ w_pa[l]) + jax.nn.sigmoid(g_b) * (y_att @ w_pb[l])
        y = merged @ w_o[l]
        x = x + gate_m * rmsnorm(y, g_post_mix[l])

        h = rmsnorm(x, g_pre_ffn[l]) * (1.0 + scale_f) + shift_f
        y = (jax.nn.silu(h @ w_ffn_gate[l]) * (h @ w_ffn_up[l])) @ w_ffn_down[l]
        x = x + gate_f * rmsnorm(y, g_post_ffn[l])
    return x
```

```python
import functools
import math

import jax
import jax.numpy as jnp
from jax import lax
from jax.experimental import pallas as pl
from jax.experimental.pallas import tpu as pltpu

F32 = jnp.float32
BF16 = jnp.bfloat16

SSM_GROUP_CH = 16
SSM_STATE = 64
HEAD_DIM = 64
N_MOD = 6
RMS_EPS = 1e-6
EIG_CLIP = 1e-4

LANES = 128
SUBLANES = 8
SLAB = 2 * LANES
VMEM_LIMIT = 56 * 1024 * 1024

TOKEN_TILE = 512
SCAN_STEPS = 64
ATTN_TILE = 512
SCAN_GROUP = 4


def _params(*sem):
    return pltpu.CompilerParams(dimension_semantics=sem, vmem_limit_bytes=VMEM_LIMIT)


def _rmsnorm(x, g):
    r = lax.rsqrt(jnp.mean(x * x, axis=-1, keepdims=True) + RMS_EPS)
    return x * r * g


def _mod_row(mod_ref, j, b):
    return mod_ref[j, pl.ds(b, 1), :]


def _mod_kernel(c_ref, w_ref, b_ref, o_ref):
    c = c_ref[...]
    cond = c * jax.nn.sigmoid(c)
    o_ref[0, 0] = jnp.dot(cond.astype(BF16), w_ref[0].astype(BF16),
                          preferred_element_type=F32) + b_ref[0, 0]


def _modulation(c, w_ada, b_ada):
    depth, d, nd = w_ada.shape
    bsz = c.shape[0]
    n = nd // d
    return pl.pallas_call(
        _mod_kernel,
        grid=(depth, n),
        in_specs=[pl.BlockSpec((bsz, d), lambda l, j: (0, 0)),
                  pl.BlockSpec((1, d, d), lambda l, j: (l, 0, j)),
                  pl.BlockSpec((1, 1, 1, d), lambda l, j: (l, j, 0, 0))],
        out_specs=pl.BlockSpec((1, 1, bsz, d), lambda l, j: (l, j, 0, 0)),
        out_shape=jax.ShapeDtypeStruct((depth, n, bsz, d), F32),
        compiler_params=_params("arbitrary", "arbitrary"),
        name="adaln_mod",
    )(c, w_ada, b_ada.reshape(depth, n, 1, d))


def _in_kernel(x_ref, mod_ref, g_ref, w_ref, wf_ref, u_ref, q_ref, k_ref, v_ref, f_ref,
               *, ssm_w, attn_w):
    b = pl.program_id(0)
    x = x_ref[0]
    h = _rmsnorm(x, g_ref[...]) * (1.0 + _mod_row(mod_ref, 1, b)) + _mod_row(mod_ref, 0, b)
    hb = h.astype(BF16)
    proj = jnp.dot(hb, w_ref[...], preferred_element_type=F32)
    u_ref[...] = proj[:, :ssm_w]
    scale = HEAD_DIM ** -0.5
    for p in range(attn_w // LANES):
        c0 = ssm_w + p * LANES
        q_ref[0, p] = (proj[:, c0:c0 + LANES] * scale).astype(BF16)
        k_ref[0, p] = proj[:, c0 + attn_w:c0 + attn_w + LANES].astype(BF16)
        v_ref[0, p] = proj[:, c0 + 2 * attn_w:c0 + 2 * attn_w + LANES].astype(BF16)
    f_ref[0] = lax.dot_general(wf_ref[...], hb, (((1,), (1,)), ((), ())),
                               preferred_element_type=F32)


def _in_projection(x, mod, g, w_main, w_f_t, ssm_w, attn_w):
    bsz, s, d = x.shape
    tm = min(TOKEN_TILE, s)
    pairs = attn_w // LANES
    heads = w_f_t.shape[0]
    kern = functools.partial(_in_kernel, ssm_w=ssm_w, attn_w=attn_w)
    qkv_shape = jax.ShapeDtypeStruct((bsz, pairs, s, LANES), BF16)
    qkv_spec = pl.BlockSpec((1, pairs, tm, LANES), lambda b, i: (b, 0, i, 0))
    return pl.pallas_call(
        kern,
        grid=(bsz, s // tm),
        in_specs=[pl.BlockSpec((1, tm, d), lambda b, i: (b, i, 0)),
                  pl.BlockSpec(mod.shape, lambda b, i: (0, 0, 0)),
                  pl.BlockSpec((1, d), lambda b, i: (0, 0)),
                  pl.BlockSpec(w_main.shape, lambda b, i: (0, 0)),
                  pl.BlockSpec(w_f_t.shape, lambda b, i: (0, 0))],
        out_specs=[pl.BlockSpec((tm, ssm_w), lambda b, i: (i, b)),
                   qkv_spec, qkv_spec, qkv_spec,
                   pl.BlockSpec((1, heads, tm), lambda b, i: (b, 0, i))],
        out_shape=[jax.ShapeDtypeStruct((s, bsz * ssm_w), F32),
                   qkv_shape, qkv_shape, qkv_shape,
                   jax.ShapeDtypeStruct((bsz, heads, s), F32)],
        compiler_params=_params("arbitrary", "arbitrary"),
        name="in_projection",
    )(x, mod, g.reshape(1, d), w_main, w_f_t)


def _cum_kernel(f_ref, bf_ref, o_ref):
    x = jax.nn.log_sigmoid(f_ref[0] + bf_ref[...])
    s = x.shape[1]
    lane = lax.broadcasted_iota(jnp.int32, x.shape, 1)
    shift = 1
    while shift < s:
        x = x + jnp.where(lane >= shift, pltpu.roll(x, shift, axis=1), 0.0)
        shift *= 2
    o_ref[0] = x


def _forget_cumsum(f_t, b_f):
    bsz, heads, s = f_t.shape
    return pl.pallas_call(
        _cum_kernel,
        grid=(bsz,),
        in_specs=[pl.BlockSpec((1, heads, s), lambda b: (b, 0, 0)),
                  pl.BlockSpec((heads, 1), lambda b: (0, 0))],
        out_specs=pl.BlockSpec((1, heads, s), lambda b: (b, 0, 0)),
        out_shape=jax.ShapeDtypeStruct((bsz, heads, s), F32),
        compiler_params=_params("arbitrary"),
        name="forget_cumsum",
    )(f_t, b_f.reshape(heads, 1))


def _ssm_kernel(u_ref, bm_ref, are_ref, aim_ref, cm_ref, dsk_ref, wg_ref, bg_ref, o_ref,
                buf, st, *, steps, n_slabs):
    @pl.when(pl.program_id(0) == 0)
    def _():
        st[...] = jnp.zeros_like(st)

    u = u_ref[...]
    ub = u.astype(BF16)
    slabs_per_block = LANES // (2 * SSM_GROUP_CH)
    for j in range(n_slabs):
        m = j // slabs_per_block
        buf[j] = jnp.dot(ub[:, m * LANES:(m + 1) * LANES], bm_ref[j],
                         preferred_element_type=F32)

    for first in range(0, n_slabs, SCAN_GROUP):
        slabs = list(range(first, first + SCAN_GROUP))
        coefs = [(are_ref[j], aim_ref[j]) for j in slabs]
        init = tuple((st[j, :, :LANES], st[j, :, LANES:]) for j in slabs)

        def body(t, carry, slabs=slabs, coefs=coefs):
            row = pl.multiple_of(t * SUBLANES, SUBLANES)
            new = []
            for (re, im), (are, aim), j in zip(carry, coefs, slabs):
                bre = buf[j, pl.ds(row, SUBLANES), :LANES]
                bim = buf[j, pl.ds(row, SUBLANES), LANES:]
                nre = are * re - aim * im + bre
                nim = are * im + aim * re + bim
                buf[j, pl.ds(row, SUBLANES), :LANES] = nre
                buf[j, pl.ds(row, SUBLANES), LANES:] = nim
                new.append((nre, nim))
            return tuple(new)

        fin = lax.fori_loop(0, steps, body, init, unroll=2)
        for (re, im), j in zip(fin, slabs):
            st[j, :, :LANES] = re
            st[j, :, LANES:] = im

    ys = []
    for m in range(n_slabs // slabs_per_block):
        acc = None
        for j in range(m * slabs_per_block, (m + 1) * slabs_per_block):
            part = jnp.dot(buf[j].astype(BF16), cm_ref[j], preferred_element_type=F32)
            acc = part if acc is None else acc + part
        ys.append(acc)
    y = jnp.concatenate(ys, axis=1) + dsk_ref[...] * u
    z = jax.nn.gelu(y, approximate=True)
    gate = jax.nn.sigmoid(
        jnp.dot(z.astype(BF16), wg_ref[...], preferred_element_type=F32) + bg_ref[...])
    o_ref[...] = (z * gate).astype(BF16)


def _ssm_branch(u_tm, bsz, bmat, a_re, a_im, cmat, d_skip, w_glu, b_glu):
    rows_total, width = u_tm.shape
    s = rows_total // bsz
    steps = min(SCAN_STEPS, s)
    rows = steps * bsz
    n_slabs = bmat.shape[0]
    kern = functools.partial(_ssm_kernel, steps=steps, n_slabs=n_slabs)
    full = lambda a: pl.BlockSpec(a.shape, lambda i: (0,) * a.ndim)
    d_skip = d_skip.reshape(1, width)
    b_glu = b_glu.reshape(1, width)
    return pl.pallas_call(
        kern,
        grid=(s // steps,),
        in_specs=[pl.BlockSpec((rows, width), lambda i: (i, 0)),
                  full(bmat), full(a_re), full(a_im), full(cmat),
                  full(d_skip), full(w_glu), full(b_glu)],
        out_specs=pl.BlockSpec((rows, width), lambda i: (i, 0)),
        out_shape=jax.ShapeDtypeStruct((rows_total, width), BF16),
        scratch_shapes=[pltpu.VMEM((n_slabs, rows, SLAB), F32),
                        pltpu.VMEM((n_slabs, bsz, SLAB), F32)],
        compiler_params=_params("arbitrary"),
        name="s5_branch",
    )(u_tm, bmat, a_re, a_im, cmat, d_skip, w_glu, b_glu)


def _s5_matrices(lam_re, lam_im, log_dt, b_re, b_im, c_re, c_im, bsz):
    g, p = lam_re.shape
    h = b_re.shape[-1]
    lr = jnp.minimum(lam_re, -EIG_CLIP)
    li = lam_im
    dt = jnp.exp(log_dt)[:, None]
    mag = jnp.exp(lr * dt)
    a_re = mag * jnp.cos(li * dt)
    a_im = mag * jnp.sin(li * dt)
    den = lr * lr + li * li
    f_re = ((a_re - 1.0) * lr + a_im * li) / den
    f_im = (a_im * lr - (a_re - 1.0) * li) / den
    bb_re = f_re[..., None] * b_re - f_im[..., None] * b_im
    bb_im = f_re[..., None] * b_im + f_im[..., None] * b_re
    n_slabs = g // 2
    per_block = LANES // (2 * h)
    n_blocks = n_slabs // per_block
    eye_s = jnp.eye(per_block, dtype=F32)
    eye_g = jnp.eye(2, dtype=F32)

    def expand(w_hp, spec):
        w = w_hp.reshape(n_blocks, per_block, 2, h, p)
        return jnp.einsum(spec, w, eye_s, eye_g)

    b_spec = "mjghp,ja,gk->mjaghkp"
    bmat = jnp.concatenate(
        [expand(bb.transpose(0, 2, 1), b_spec).reshape(n_slabs, LANES, LANES)
         for bb in (bb_re, bb_im)], axis=2)
    c_spec = "mjghp,ja,gk->mjkpagh"
    cmat = jnp.concatenate(
        [expand(cc, c_spec).reshape(n_slabs, LANES, LANES) for cc in (c_re, -c_im)], axis=1)
    a_re_s = a_re.reshape(n_slabs, 1, 2 * p)
    a_im_s = a_im.reshape(n_slabs, 1, 2 * p)
    a_re_s = jnp.broadcast_to(a_re_s, (n_slabs, bsz, 2 * p))
    a_im_s = jnp.broadcast_to(a_im_s, (n_slabs, bsz, 2 * p))
    return bmat.astype(BF16), a_re_s, a_im_s, cmat.astype(BF16)


def _attn_kernel(q_ref, k_ref, v_ref, crow_ref, ccol_ref, o_ref, m_sc, l_sc, acc_sc, *, tile):
    i = pl.program_id(2)
    q = q_ref[0, 0].astype(F32)
    lane = lax.broadcasted_iota(jnp.int32, q.shape, 1)
    q_heads = (jnp.where(lane < HEAD_DIM, q, 0.0).astype(BF16),
               jnp.where(lane >= HEAD_DIM, q, 0.0).astype(BF16))
    cb = (ccol_ref[0, 0, :, 0:1], ccol_ref[0, 0, :, 1:2])
    m_sc[...] = jnp.full_like(m_sc, -jnp.inf)
    l_sc[...] = jnp.zeros_like(l_sc)
    acc_sc[...] = jnp.zeros_like(acc_sc)

    def step(j, masked):
        ks = pl.multiple_of(j * tile, tile)
        kt = k_ref[0, 0, pl.ds(ks, tile), :]
        vt = v_ref[0, 0, pl.ds(ks, tile), :]
        for hh in range(2):
            s = lax.dot_general(q_heads[hh], kt, (((1,), (1,)), ((), ())),
                                preferred_element_type=F32)
            ck = crow_ref[0, 0, pl.ds(hh, 1), pl.ds(ks, tile)]
            s = s + cb[hh] - ck
            if masked:
                r = lax.broadcasted_iota(jnp.int32, s.shape, 0)
                c = lax.broadcasted_iota(jnp.int32, s.shape, 1)
                s = jnp.where(c <= r, s, -jnp.inf)
            m_old = m_sc[hh]
            m_new = jnp.maximum(m_old, jnp.max(s, axis=-1, keepdims=True))
            alpha = jnp.exp(m_old - m_new)
            pexp = jnp.exp(s - m_new)
            l_sc[hh] = alpha * l_sc[hh] + jnp.sum(pexp, axis=-1, keepdims=True)
            acc_sc[hh] = alpha * acc_sc[hh] + jnp.dot(pexp.astype(BF16), vt,
                                                      preferred_element_type=F32)
            m_sc[hh] = m_new

    def off_diagonal(j, carry):
        step(j, False)
        return carry

    lax.fori_loop(0, i, off_diagonal, 0)
    step(i, True)
    out = jnp.where(lane < HEAD_DIM, acc_sc[0] / l_sc[0], acc_sc[1] / l_sc[1])
    o_ref[0] = out.astype(BF16)


def _attention(q, k, v, cum_row, cum_col):
    bsz, pairs, s, _ = q.shape
    tile = min(ATTN_TILE, s)
    kern = functools.partial(_attn_kernel, tile=tile)
    return pl.pallas_call(
        kern,
        grid=(bsz, pairs, s // tile),
        in_specs=[pl.BlockSpec((1, 1, tile, LANES), lambda b, p, i: (b, p, i, 0)),
                  pl.BlockSpec((1, 1, s, LANES), lambda b, p, i: (b, p, 0, 0)),
                  pl.BlockSpec((1, 1, s, LANES), lambda b, p, i: (b, p, 0, 0)),
                  pl.BlockSpec((1, 1, 2, s), lambda b, p, i: (b, p, 0, 0)),
                  pl.BlockSpec((1, 1, tile, 2), lambda b, p, i: (b, p, i, 0))],
        out_specs=pl.BlockSpec((1, tile, LANES), lambda b, p, i: (b, i, p)),
        out_shape=jax.ShapeDtypeStruct((bsz, s, pairs * LANES), BF16),
        scratch_shapes=[pltpu.VMEM((2, tile, 1), F32),
                        pltpu.VMEM((2, tile, 1), F32),
                        pltpu.VMEM((2, tile, LANES), F32)],
        compiler_params=_params("arbitrary", "arbitrary", "arbitrary"),
        name="forgetting_attention",
    )(q, k, v, cum_row, cum_col)


def _merge_kernel(x_ref, mod_ref, gpre_ref, gpost_ref, wgate_ref, ys_ref, ya_ref,
                  wpa_ref, wpb_ref, wo_ref, o_ref):
    b = pl.program_id(0)
    x = x_ref[0]
    d = x.shape[-1]
    h = _rmsnorm(x, gpre_ref[...]) * (1.0 + _mod_row(mod_ref, 1, b)) + _mod_row(mod_ref, 0, b)
    gates = jnp.dot(h.astype(BF16), wgate_ref[...], preferred_element_type=F32)
    pa = jnp.dot(ys_ref[...], wpa_ref[...], preferred_element_type=F32)
    pb = jnp.dot(ya_ref[0], wpb_ref[...], preferred_element_type=F32)
    merged = jax.nn.sigmoid(gates[:, :d]) * pa + jax.nn.sigmoid(gates[:, d:]) * pb
    y = jnp.dot(merged.astype(BF16), wo_ref[...], preferred_element_type=F32)
    o_ref[0] = x + _mod_row(mod_ref, 2, b) * _rmsnorm(y, gpost_ref[...])


def _merge(x, mod, g_pre, g_post, w_gate, y_ssm_tm, y_att, w_pa, w_pb, w_o):
    bsz, s, d = x.shape
    tm = min(TOKEN_TILE, s)
    width = y_att.shape[-1]
    full = lambda a: pl.BlockSpec(a.shape, lambda b, i: (0,) * a.ndim)
    g_pre = g_pre.reshape(1, d)
    g_post = g_post.reshape(1, d)
    return pl.pallas_call(
        _merge_kernel,
        grid=(bsz, s // tm),
        in_specs=[pl.BlockSpec((1, tm, d), lambda b, i: (b, i, 0)),
                  full(mod), full(g_pre), full(g_post), full(w_gate),
                  pl.BlockSpec((tm, width), lambda b, i: (i, b)),
                  pl.BlockSpec((1, tm, width), lambda b, i: (b, i, 0)),
                  full(w_pa), full(w_pb), full(w_o)],
        out_specs=pl.BlockSpec((1, tm, d), lambda b, i: (b, i, 0)),
        out_shape=jax.ShapeDtypeStruct(x.shape, F32),
        compiler_params=_params("arbitrary", "arbitrary"),
        name="gated_merge",
    )(x, mod, g_pre, g_post, w_gate, y_ssm_tm, y_att, w_pa, w_pb, w_o)


def _ffn_chunks(d_ff):
    mxu = 2 * LANES
    n_tiles = d_ff // mxu
    first = (n_tiles + 1) // 2 * mxu
    return ((0, first), (first, d_ff)) if first < d_ff else ((0, d_ff),)


def _ffn_kernel(x_ref, mod_ref, gpre_ref, gpost_ref, wg_ref, wu_ref, wd_ref, o_ref, *, chunks):
    b = pl.program_id(0)
    x = x_ref[0]
    h = _rmsnorm(x, gpre_ref[...]) * (1.0 + _mod_row(mod_ref, 4, b)) + _mod_row(mod_ref, 3, b)
    hb = h.astype(BF16)
    acc = None
    for c0, c1 in chunks:
        g = jnp.dot(hb, wg_ref[:, c0:c1], preferred_element_type=F32)
        up = jnp.dot(hb, wu_ref[:, c0:c1], preferred_element_type=F32)
        a = (g * jax.nn.sigmoid(g) * up).astype(BF16)
        part = jnp.dot(a, wd_ref[c0:c1, :], preferred_element_type=F32)
        acc = part if acc is None else acc + part
    o_ref[0] = x + _mod_row(mod_ref, 5, b) * _rmsnorm(acc, gpost_ref[...])


def _ffn(x, mod, g_pre, g_post, w_gate, w_up, w_down):
    bsz, s, d = x.shape
    tm = min(TOKEN_TILE, s)
    kern = functools.partial(_ffn_kernel, chunks=_ffn_chunks(w_gate.shape[1]))
    full = lambda a: pl.BlockSpec(a.shape, lambda b, i: (0,) * a.ndim)
    g_pre = g_pre.reshape(1, d)
    g_post = g_post.reshape(1, d)
    return pl.pallas_call(
        kern,
        grid=(bsz, s // tm),
        in_specs=[pl.BlockSpec((1, tm, d), lambda b, i: (b, i, 0)),
                  full(mod), full(g_pre), full(g_post),
                  full(w_gate), full(w_up), full(w_down)],
        out_specs=pl.BlockSpec((1, tm, d), lambda b, i: (b, i, 0)),
        out_shape=jax.ShapeDtypeStruct(x.shape, F32),
        compiler_params=_params("arbitrary", "arbitrary"),
        name="swiglu_ffn",
    )(x, mod, g_pre, g_post, w_gate, w_up, w_down)


def kernel(x, c, w_ada, b_ada, g_pre_mix, g_post_mix, g_pre_ffn, g_post_ffn, w_in, lam_re, lam_im, log_dt, b_re, b_im, c_re, c_im, d_skip, w_glu, b_glu, b_f, w_pa, w_pb, w_o, w_ffn_gate, w_ffn_up, w_ffn_down):
    bsz, s, d = x.shape
    depth = w_in.shape[0]
    ssm_w = d_skip.shape[1]
    heads = b_f.shape[1]
    attn_w = heads * HEAD_DIM
    pairs = attn_w // LANES
    main_w = ssm_w + 3 * attn_w
    assert bsz == SUBLANES, "the S5 scan keeps the batch on the sublane axis"

    mod_all = _modulation(c, w_ada, b_ada)
    for l in range(depth):
        mod = mod_all[l]
        w_main = w_in[l, :, :main_w].astype(BF16)
        w_f_t = w_in[l, :, main_w:main_w + heads].T.astype(BF16)
        w_gate = w_in[l, :, main_w + heads:].astype(BF16)

        u_tm, q, k, v, f_t = _in_projection(x, mod, g_pre_mix[l], w_main, w_f_t, ssm_w, attn_w)

        cum = _forget_cumsum(f_t, b_f[l])
        cum_row = cum.reshape(bsz, pairs, 2, s)
        cum_col = cum_row.transpose(0, 1, 3, 2)
        y_att = _attention(q, k, v, cum_row, cum_col)

        bmat, a_re, a_im, cmat = _s5_matrices(lam_re[l], lam_im[l], log_dt[l], b_re[l], b_im[l],
                                              c_re[l], c_im[l], bsz)
        y_ssm = _ssm_branch(u_tm.reshape(s * bsz, ssm_w), bsz, bmat, a_re, a_im, cmat,
                            d_skip[l], w_glu[l].astype(BF16), b_glu[l])
        y_ssm = y_ssm.reshape(s, bsz * ssm_w)

        x = _merge(x, mod, g_pre_mix[l], g_post_mix[l], w_gate, y_ssm, y_att,
                   w_pa[l].astype(BF16), w_pb[l].astype(BF16), w_o[l].astype(BF16))
        x = _ffn(x, mod, g_pre_ffn[l], g_post_ffn[l], w_ffn_gate[l].astype(BF16),
                 w_ffn_up[l].astype(BF16), w_ffn_down[l].astype(BF16))
    return x
```

```python
import functools
import math

import jax
import jax.numpy as jnp
from jax import lax
from jax.experimental import pallas as pl
from jax.experimental.pallas import tpu as pltpu

F32 = jnp.float32
BF16 = jnp.bfloat16

SSM_GROUP_CH = 16
SSM_STATE = 64
HEAD_DIM = 64
N_MOD = 6
RMS_EPS = 1e-6
EIG_CLIP = 1e-4
LOG2E = math.log2(math.e)

LANES = 128
SUBLANES = 8
SLAB = 2 * LANES
VMEM_LIMIT = 56 * 1024 * 1024

TOKEN_TILE = 512
SCAN_STEPS = 64
ATTN_Q_TILE = 256
ATTN_K_TILE = 512
SCAN_GROUP = 4


def _params(*sem):
    return pltpu.CompilerParams(dimension_semantics=sem, vmem_limit_bytes=VMEM_LIMIT)


def _rmsnorm(x, g):
    r = lax.rsqrt(jnp.mean(x * x, axis=-1, keepdims=True) + RMS_EPS)
    return x * r * g


def _mod_row(mod_ref, j, b):
    return mod_ref[j, pl.ds(b, 1), :]


def _mod_kernel(c_ref, w_ref, b_ref, o_ref):
    c = c_ref[...]
    cond = c * jax.nn.sigmoid(c)
    o_ref[0, 0] = jnp.dot(cond.astype(BF16), w_ref[0].astype(BF16),
                          preferred_element_type=F32) + b_ref[0, 0]


def _modulation(c, w_ada, b_ada):
    depth, d, nd = w_ada.shape
    bsz = c.shape[0]
    n = nd // d
    return pl.pallas_call(
        _mod_kernel,
        grid=(depth, n),
        in_specs=[pl.BlockSpec((bsz, d), lambda l, j: (0, 0)),
                  pl.BlockSpec((1, d, d), lambda l, j: (l, 0, j)),
                  pl.BlockSpec((1, 1, 1, d), lambda l, j: (l, j, 0, 0))],
        out_specs=pl.BlockSpec((1, 1, bsz, d), lambda l, j: (l, j, 0, 0)),
        out_shape=jax.ShapeDtypeStruct((depth, n, bsz, d), F32),
        compiler_params=_params("arbitrary", "arbitrary"),
        name="adaln_mod",
    )(c, w_ada, b_ada.reshape(depth, n, 1, d))


def _in_kernel(x_ref, mod_ref, g_ref, w_ref, wvf_ref, u_ref, q_ref, k_ref, vt_ref, f_ref,
               *, ssm_w, attn_w):
    b = pl.program_id(0)
    x = x_ref[0]
    h = _rmsnorm(x, g_ref[...]) * (1.0 + _mod_row(mod_ref, 1, b)) + _mod_row(mod_ref, 0, b)
    hb = h.astype(BF16)
    proj = jnp.dot(hb, w_ref[...], preferred_element_type=F32)
    u_ref[...] = proj[:, :ssm_w]
    scale = HEAD_DIM ** -0.5 * LOG2E
    proj_t = lax.dot_general(wvf_ref[...], hb, (((1,), (1,)), ((), ())),
                             preferred_element_type=F32)
    for p in range(attn_w // LANES):
        c0 = ssm_w + p * LANES
        q_ref[0, p] = (proj[:, c0:c0 + LANES] * scale).astype(BF16)
        k_ref[0, p] = proj[:, c0 + attn_w:c0 + attn_w + LANES].astype(BF16)
        vt_ref[0, p] = proj_t[p * LANES:(p + 1) * LANES, :].astype(BF16)
    f_ref[0] = proj_t[attn_w:, :]


def _in_projection(x, mod, g, w_main, w_vf_t, ssm_w, attn_w):
    bsz, s, d = x.shape
    tm = min(TOKEN_TILE, s)
    pairs = attn_w // LANES
    heads = w_vf_t.shape[0] - attn_w
    kern = functools.partial(_in_kernel, ssm_w=ssm_w, attn_w=attn_w)
    qk_shape = jax.ShapeDtypeStruct((bsz, pairs, s, LANES), BF16)
    qk_spec = pl.BlockSpec((1, pairs, tm, LANES), lambda b, i: (b, 0, i, 0))
    return pl.pallas_call(
        kern,
        grid=(bsz, s // tm),
        in_specs=[pl.BlockSpec((1, tm, d), lambda b, i: (b, i, 0)),
                  pl.BlockSpec(mod.shape, lambda b, i: (0, 0, 0)),
                  pl.BlockSpec((1, d), lambda b, i: (0, 0)),
                  pl.BlockSpec(w_main.shape, lambda b, i: (0, 0)),
                  pl.BlockSpec(w_vf_t.shape, lambda b, i: (0, 0))],
        out_specs=[pl.BlockSpec((tm, ssm_w), lambda b, i: (i, b)),
                   qk_spec, qk_spec,
                   pl.BlockSpec((1, pairs, LANES, tm), lambda b, i: (b, 0, 0, i)),
                   pl.BlockSpec((1, heads, tm), lambda b, i: (b, 0, i))],
        out_shape=[jax.ShapeDtypeStruct((s, bsz * ssm_w), F32),
                   qk_shape, qk_shape,
                   jax.ShapeDtypeStruct((bsz, pairs, LANES, s), BF16),
                   jax.ShapeDtypeStruct((bsz, heads, s), F32)],
        compiler_params=_params("arbitrary", "arbitrary"),
        name="in_projection",
    )(x, mod, g.reshape(1, d), w_main, w_vf_t)


def _cum_kernel(f_ref, bf_ref, o_ref):
    x = jax.nn.log_sigmoid(f_ref[0] + bf_ref[...])
    s = x.shape[1]
    lane = lax.broadcasted_iota(jnp.int32, x.shape, 1)
    shift = 1
    while shift < s:
        x = x + jnp.where(lane >= shift, pltpu.roll(x, shift, axis=1), 0.0)
        shift *= 2
    o_ref[0] = x * LOG2E


def _forget_cumsum(f_t, b_f):
    bsz, heads, s = f_t.shape
    return pl.pallas_call(
        _cum_kernel,
        grid=(bsz,),
        in_specs=[pl.BlockSpec((1, heads, s), lambda b: (b, 0, 0)),
                  pl.BlockSpec((heads, 1), lambda b: (0, 0))],
        out_specs=pl.BlockSpec((1, heads, s), lambda b: (b, 0, 0)),
        out_shape=jax.ShapeDtypeStruct((bsz, heads, s), F32),
        compiler_params=_params("arbitrary"),
        name="forget_cumsum",
    )(f_t, b_f.reshape(heads, 1))


def _ssm_kernel(u_ref, bm_ref, are_ref, aim_ref, cm_ref, dsk_ref, wg_ref, bg_ref, o_ref,
                buf, st, *, steps, n_slabs):
    @pl.when(pl.program_id(0) == 0)
    def _():
        st[...] = jnp.zeros_like(st)

    u = u_ref[...]
    ub = u.astype(BF16)
    slabs_per_block = LANES // (2 * SSM_GROUP_CH)
    for j in range(n_slabs):
        m = j // slabs_per_block
        buf[j] = jnp.dot(ub[:, m * LANES:(m + 1) * LANES], bm_ref[j],
                         preferred_element_type=F32)

    for first in range(0, n_slabs, SCAN_GROUP):
        slabs = list(range(first, first + SCAN_GROUP))
        coefs = [(are_ref[j], aim_ref[j]) for j in slabs]
        init = tuple((st[j, :, :LANES], st[j, :, LANES:]) for j in slabs)

        def body(t, carry, slabs=slabs, coefs=coefs):
            row = pl.multiple_of(t * SUBLANES, SUBLANES)
            new = []
            for (re, im), (are, aim), j in zip(carry, coefs, slabs):
                bre = buf[j, pl.ds(row, SUBLANES), :LANES]
                bim = buf[j, pl.ds(row, SUBLANES), LANES:]
                nre = are * re - aim * im + bre
                nim = are * im + aim * re + bim
                buf[j, pl.ds(row, SUBLANES), :LANES] = nre
                buf[j, pl.ds(row, SUBLANES), LANES:] = nim
                new.append((nre, nim))
            return tuple(new)

        fin = lax.fori_loop(0, steps, body, init, unroll=2)
        for (re, im), j in zip(fin, slabs):
            st[j, :, :LANES] = re
            st[j, :, LANES:] = im

    ys = []
    for m in range(n_slabs // slabs_per_block):
        acc = None
        for j in range(m * slabs_per_block, (m + 1) * slabs_per_block):
            part = jnp.dot(buf[j].astype(BF16), cm_ref[j], preferred_element_type=F32)
            acc = part if acc is None else acc + part
        ys.append(acc)
    y = jnp.concatenate(ys, axis=1) + dsk_ref[...] * u
    z = jax.nn.gelu(y, approximate=True)
    gate = jax.nn.sigmoid(
        jnp.dot(z.astype(BF16), wg_ref[...], preferred_element_type=F32) + bg_ref[...])
    o_ref[...] = (z * gate).astype(BF16)


def _ssm_branch(u_tm, bsz, bmat, a_re, a_im, cmat, d_skip, w_glu, b_glu):
    rows_total, width = u_tm.shape
    s = rows_total // bsz
    steps = min(SCAN_STEPS, s)
    rows = steps * bsz
    n_slabs = bmat.shape[0]
    kern = functools.partial(_ssm_kernel, steps=steps, n_slabs=n_slabs)
    full = lambda a: pl.BlockSpec(a.shape, lambda i: (0,) * a.ndim)
    d_skip = d_skip.reshape(1, width)
    b_glu = b_glu.reshape(1, width)
    return pl.pallas_call(
        kern,
        grid=(s // steps,),
        in_specs=[pl.BlockSpec((rows, width), lambda i: (i, 0)),
                  full(bmat), full(a_re), full(a_im), full(cmat),
                  full(d_skip), full(w_glu), full(b_glu)],
        out_specs=pl.BlockSpec((rows, width), lambda i: (i, 0)),
        out_shape=jax.ShapeDtypeStruct((rows_total, width), BF16),
        scratch_shapes=[pltpu.VMEM((n_slabs, rows, SLAB), F32),
                        pltpu.VMEM((n_slabs, bsz, SLAB), F32)],
        compiler_params=_params("arbitrary"),
        name="s5_branch",
    )(u_tm, bmat, a_re, a_im, cmat, d_skip, w_glu, b_glu)


def _s5_matrices(lam_re, lam_im, log_dt, b_re, b_im, c_re, c_im, bsz):
    g, p = lam_re.shape
    h = b_re.shape[-1]
    lr = jnp.minimum(lam_re, -EIG_CLIP)
    li = lam_im
    dt = jnp.exp(log_dt)[:, None]
    mag = jnp.exp(lr * dt)
    a_re = mag * jnp.cos(li * dt)
    a_im = mag * jnp.sin(li * dt)
    den = lr * lr + li * li
    f_re = ((a_re - 1.0) * lr + a_im * li) / den
    f_im = (a_im * lr - (a_re - 1.0) * li) / den
    bb_re = f_re[..., None] * b_re - f_im[..., None] * b_im
    bb_im = f_re[..., None] * b_im + f_im[..., None] * b_re
    n_slabs = g // 2
    per_block = LANES // (2 * h)
    n_blocks = n_slabs // per_block
    eye_s = jnp.eye(per_block, dtype=F32)
    eye_g = jnp.eye(2, dtype=F32)

    def expand(w_hp, spec):
        w = w_hp.reshape(n_blocks, per_block, 2, h, p)
        return jnp.einsum(spec, w, eye_s, eye_g)

    b_spec = "mjghp,ja,gk->mjaghkp"
    bmat = jnp.concatenate(
        [expand(bb.transpose(0, 2, 1), b_spec).reshape(n_slabs, LANES, LANES)
         for bb in (bb_re, bb_im)], axis=2)
    c_spec = "mjghp,ja,gk->mjkpagh"
    cmat = jnp.concatenate(
        [expand(cc, c_spec).reshape(n_slabs, LANES, LANES) for cc in (c_re, -c_im)], axis=1)
    a_re_s = a_re.reshape(n_slabs, 1, 2 * p)
    a_im_s = a_im.reshape(n_slabs, 1, 2 * p)
    a_re_s = jnp.broadcast_to(a_re_s, (n_slabs, bsz, 2 * p))
    a_im_s = jnp.broadcast_to(a_im_s, (n_slabs, bsz, 2 * p))
    return bmat.astype(BF16), a_re_s, a_im_s, cmat.astype(BF16)


def _attn_kernel(q_ref, k_ref, vt_ref, ccol_ref, o_ref,
                 s_buf, p_buf, a_buf, m_sc, l_sc, acc_sc, *, tq, tk):
    i = pl.program_id(2)
    qs = pl.multiple_of(i * tq, tq)
    q = q_ref[0, 0].astype(F32)
    lane = lax.broadcasted_iota(jnp.int32, q.shape, 1)
    q_heads = (jnp.where(lane < HEAD_DIM, q, 0.0).astype(BF16),
               jnp.where(lane >= HEAD_DIM, q, 0.0).astype(BF16))
    m_sc[...] = jnp.full_like(m_sc, -jnp.inf)
    l_sc[...] = jnp.zeros_like(l_sc)
    acc_sc[...] = jnp.zeros_like(acc_sc)
    p_buf[1] = jnp.zeros(p_buf.shape[1:], BF16)
    a_buf[1] = jnp.ones(a_buf.shape[1:], F32)

    def scores(j, slot):
        ks = pl.multiple_of(j * tk, tk)
        kt = k_ref[0, 0, pl.ds(ks, tk), :]
        for hh in range(2):
            s_buf[slot, hh] = lax.dot_general(kt, q_heads[hh], (((1,), (1,)), ((), ())),
                                              preferred_element_type=F32)

    def softmax(j, slot, masked):
        ks = pl.multiple_of(j * tk, tk)
        for hh in range(2):
            ck = ccol_ref[0, 0, pl.ds(ks, tk), hh:hh + 1]
            s = s_buf[slot, hh] - ck
            if masked:
                kpos = ks + lax.broadcasted_iota(jnp.int32, s.shape, 0)
                qpos = qs + lax.broadcasted_iota(jnp.int32, s.shape, 1)
                s = jnp.where(kpos <= qpos, s, -jnp.inf)
            m_old = m_sc[hh]
            m_new = jnp.maximum(m_old, jnp.max(s, axis=0, keepdims=True))
            alpha = jnp.exp2(m_old - m_new)
            pexp = jnp.exp2(s - m_new)
            l_sc[hh] = alpha * l_sc[hh] + jnp.sum(pexp, axis=0, keepdims=True)
            m_sc[hh] = m_new
            a_buf[slot, hh] = alpha
            p_buf[slot, hh] = pexp.astype(BF16)

    def values(j, slot):
        ks = pl.multiple_of(j * tk, tk)
        vt = vt_ref[0, 0, :, pl.ds(ks, tk)]
        for hh in range(2):
            rows = slice(hh * HEAD_DIM, (hh + 1) * HEAD_DIM)
            acc_sc[rows, :] = a_buf[slot, hh] * acc_sc[rows, :] + jnp.dot(
                vt[rows, :], p_buf[slot, hh], preferred_element_type=F32)

    n_full = qs // tk
    n_pairs = n_full // 2
    scores(0, 0)

    def tile_pair(t, carry):
        j = 2 * t
        scores(j + 1, 1)
        softmax(j, 0, False)
        values(jnp.maximum(j - 1, 0), 1)
        scores(j + 2, 0)
        softmax(j + 1, 1, False)
        values(j, 0)
        return carry

    lax.fori_loop(0, n_pairs, tile_pair, 0)
    j0 = 2 * n_pairs

    @pl.when(n_full % 2 == 1)
    def _():
        scores(j0 + 1, 1)
        softmax(j0, 0, False)
        values(jnp.maximum(j0 - 1, 0), 1)
        softmax(j0 + 1, 1, True)
        values(j0, 0)
        values(j0 + 1, 1)

    @pl.when(n_full % 2 == 0)
    def _():
        softmax(j0, 0, True)
        values(jnp.maximum(j0 - 1, 0), 1)
        values(j0, 0)

    out_t = jnp.concatenate([acc_sc[:HEAD_DIM, :] / l_sc[0], acc_sc[HEAD_DIM:, :] / l_sc[1]], axis=0)
    o_ref[0] = out_t.T.astype(BF16)


def _attention(q, k, v_t, cum_col):
    bsz, pairs, s, _ = q.shape
    tq = min(ATTN_Q_TILE, s)
    tk = min(ATTN_K_TILE, s)
    assert tk % tq == 0
    kern = functools.partial(_attn_kernel, tq=tq, tk=tk)
    return pl.pallas_call(
        kern,
        grid=(bsz, pairs, s // tq),
        in_specs=[pl.BlockSpec((1, 1, tq, LANES), lambda b, p, i: (b, p, i, 0)),
                  pl.BlockSpec((1, 1, s, LANES), lambda b, p, i: (b, p, 0, 0)),
                  pl.BlockSpec((1, 1, LANES, s), lambda b, p, i: (b, p, 0, 0)),
                  pl.BlockSpec((1, 1, s, 2), lambda b, p, i: (b, p, 0, 0))],
        out_specs=pl.BlockSpec((1, tq, LANES), lambda b, p, i: (b, i, p)),
        out_shape=jax.ShapeDtypeStruct((bsz, s, pairs * LANES), BF16),
        scratch_shapes=[pltpu.VMEM((2, 2, tk, tq), F32),
                        pltpu.VMEM((2, 2, tk, tq), BF16),
                        pltpu.VMEM((2, 2, 1, tq), F32),
                        pltpu.VMEM((2, 1, tq), F32),
                        pltpu.VMEM((2, 1, tq), F32),
                        pltpu.VMEM((LANES, tq), F32)],
        compiler_params=_params("arbitrary", "arbitrary", "arbitrary"),
        name="forgetting_attention",
    )(q, k, v_t, cum_col)


def _merge_kernel(x_ref, mod_ref, gpre_ref, gpost_ref, wgate_ref, ys_ref, ya_ref,
                  wpa_ref, wpb_ref, wo_ref, o_ref):
    b = pl.program_id(0)
    x = x_ref[0]
    d = x.shape[-1]
    h = _rmsnorm(x, gpre_ref[...]) * (1.0 + _mod_row(mod_ref, 1, b)) + _mod_row(mod_ref, 0, b)
    gates = jnp.dot(h.astype(BF16), wgate_ref[...], preferred_element_type=F32)
    pa = jnp.dot(ys_ref[...], wpa_ref[...], preferred_element_type=F32)
    pb = jnp.dot(ya_ref[0], wpb_ref[...], preferred_element_type=F32)
    merged = jax.nn.sigmoid(gates[:, :d]) * pa + jax.nn.sigmoid(gates[:, d:]) * pb
    y = jnp.dot(merged.astype(BF16), wo_ref[...], preferred_element_type=F32)
    o_ref[0] = x + _mod_row(mod_ref, 2, b) * _rmsnorm(y, gpost_ref[...])


def _merge(x, mod, g_pre, g_post, w_gate, y_ssm_tm, y_att, w_pa, w_pb, w_o):
    bsz, s, d = x.shape
    tm = min(TOKEN_TILE, s)
    width = y_att.shape[-1]
    full = lambda a: pl.BlockSpec(a.shape, lambda b, i: (0,) * a.ndim)
    g_pre = g_pre.reshape(1, d)
    g_post = g_post.reshape(1, d)
    return pl.pallas_call(
        _merge_kernel,
        grid=(bsz, s // tm),
        in_specs=[pl.BlockSpec((1, tm, d), lambda b, i: (b, i, 0)),
                  full(mod), full(g_pre), full(g_post), full(w_gate),
                  pl.BlockSpec((tm, width), lambda b, i: (i, b)),
                  pl.BlockSpec((1, tm, width), lambda b, i: (b, i, 0)),
                  full(w_pa), full(w_pb), full(w_o)],
        out_specs=pl.BlockSpec((1, tm, d), lambda b, i: (b, i, 0)),
        out_shape=jax.ShapeDtypeStruct(x.shape, F32),
        compiler_params=_params("arbitrary", "arbitrary"),
        name="gated_merge",
    )(x, mod, g_pre, g_post, w_gate, y_ssm_tm, y_att, w_pa, w_pb, w_o)


def _ffn_chunks(d_ff):
    mxu = 2 * LANES
    n_tiles = d_ff // mxu
    first = (n_tiles + 1) // 2 * mxu
    return ((0, first), (first, d_ff)) if first < d_ff else ((0, d_ff),)


def _ffn_kernel(x_ref, mod_ref, gpre_ref, gpost_ref, wg_ref, wu_ref, wd_ref, o_ref, *, chunks):
    b = pl.program_id(0)
    x = x_ref[0]
    h = _rmsnorm(x, gpre_ref[...]) * (1.0 + _mod_row(mod_ref, 4, b)) + _mod_row(mod_ref, 3, b)
    hb = h.astype(BF16)
    acc = None
    for c0, c1 in chunks:
        g = jnp.dot(hb, wg_ref[:, c0:c1], preferred_element_type=F32)
        up = jnp.dot(hb, wu_ref[:, c0:c1], preferred_element_type=F32)
        a = (g * jax.nn.sigmoid(g) * up).astype(BF16)
        part = jnp.dot(a, wd_ref[c0:c1, :], preferred_element_type=F32)
        acc = part if acc is None else acc + part
    o_ref[0] = x + _mod_row(mod_ref, 5, b) * _rmsnorm(acc, gpost_ref[...])


def _ffn(x, mod, g_pre, g_post, w_gate, w_up, w_down):
    bsz, s, d = x.shape
    tm = min(TOKEN_TILE, s)
    kern = functools.partial(_ffn_kernel, chunks=_ffn_chunks(w_gate.shape[1]))
    full = lambda a: pl.BlockSpec(a.shape, lambda b, i: (0,) * a.ndim)
    g_pre = g_pre.reshape(1, d)
    g_post = g_post.reshape(1, d)
    return pl.pallas_call(
        kern,
        grid=(bsz, s // tm),
        in_specs=[pl.BlockSpec((1, tm, d), lambda b, i: (b, i, 0)),
                  full(mod), full(g_pre), full(g_post),
                  full(w_gate), full(w_up), full(w_down)],
        out_specs=pl.BlockSpec((1, tm, d), lambda b, i: (b, i, 0)),
        out_shape=jax.ShapeDtypeStruct(x.shape, F32),
        compiler_params=_params("arbitrary", "arbitrary"),
        name="swiglu_ffn",
    )(x, mod, g_pre, g_post, w_gate, w_up, w_down)


def kernel(x, c, w_ada, b_ada, g_pre_mix, g_post_mix, g_pre_ffn, g_post_ffn, w_in, lam_re, lam_im, log_dt, b_re, b_im, c_re, c_im, d_skip, w_glu, b_glu, b_f, w_pa, w_pb, w_o, w_ffn_gate, w_ffn_up, w_ffn_down):
    bsz, s, d = x.shape
    depth = w_in.shape[0]
    ssm_w = d_skip.shape[1]
    heads = b_f.shape[1]
    attn_w = heads * HEAD_DIM
    pairs = attn_w // LANES
    main_w = ssm_w + 3 * attn_w
    assert bsz == SUBLANES, "the S5 scan keeps the batch on the sublane axis"

    mod_all = _modulation(c, w_ada, b_ada)
    for l in range(depth):
        mod = mod_all[l]
        uqk_w = ssm_w + 2 * attn_w
        w_main = w_in[l, :, :uqk_w].astype(BF16)
        w_vf_t = w_in[l, :, uqk_w:main_w + heads].T.astype(BF16)
        w_gate = w_in[l, :, main_w + heads:].astype(BF16)

        u_tm, q, k, v_t, f_t = _in_projection(x, mod, g_pre_mix[l], w_main, w_vf_t, ssm_w, attn_w)

        cum = _forget_cumsum(f_t, b_f[l])
        cum_col = cum.reshape(bsz, pairs, 2, s).transpose(0, 1, 3, 2)
        y_att = _attention(q, k, v_t, cum_col)

        bmat, a_re, a_im, cmat = _s5_matrices(lam_re[l], lam_im[l], log_dt[l], b_re[l], b_im[l],
                                              c_re[l], c_im[l], bsz)
        y_ssm = _ssm_branch(u_tm.reshape(s * bsz, ssm_w), bsz, bmat, a_re, a_im, cmat,
                            d_skip[l], w_glu[l].astype(BF16), b_glu[l])
        y_ssm = y_ssm.reshape(s, bsz * ssm_w)

        x = _merge(x, mod, g_pre_mix[l], g_post_mix[l], w_gate, y_ssm, y_att,
                   w_pa[l].astype(BF16), w_pb[l].astype(BF16), w_o[l].astype(BF16))
        x = _ffn(x, mod, g_pre_ffn[l], g_post_ffn[l], w_ffn_gate[l].astype(BF16),
                 w_ffn_up[l].astype(BF16), w_ffn_down[l].astype(BF16))
    return x
```

```python
import functools
import math

import jax
import jax.numpy as jnp
from jax import lax
from jax.experimental import pallas as pl
from jax.experimental.pallas import tpu as pltpu

F32 = jnp.float32
BF16 = jnp.bfloat16

SSM_GROUP_CH = 16
SSM_STATE = 64
HEAD_DIM = 64
N_MOD = 6
RMS_EPS = 1e-6
EIG_CLIP = 1e-4
LOG2E = math.log2(math.e)

LANES = 128
SUBLANES = 8
SLAB = 2 * LANES
VMEM_LIMIT = 56 * 1024 * 1024

TOKEN_TILE = 512
SCAN_STEPS = 64
ATTN_Q_TILE = 256
ATTN_K_TILE = 512
SCAN_GROUP = 4
SCAN_UNROLL = 4


def _params(*sem):
    return pltpu.CompilerParams(dimension_semantics=sem, vmem_limit_bytes=VMEM_LIMIT)


def _rmsnorm(x, g):
    r = lax.rsqrt(jnp.mean(x * x, axis=-1, keepdims=True) + RMS_EPS)
    return x * r * g


def _mod_row(mod_ref, j, b):
    return mod_ref[j, pl.ds(b, 1), :]


def _mod_kernel(c_ref, w_ref, b_ref, o_ref):
    c = c_ref[...]
    cond = c * jax.nn.sigmoid(c)
    o_ref[0, 0] = jnp.dot(cond.astype(BF16), w_ref[0].astype(BF16),
                          preferred_element_type=F32) + b_ref[0, 0]


def _modulation(c, w_ada, b_ada):
    depth, d, nd = w_ada.shape
    bsz = c.shape[0]
    n = nd // d
    return pl.pallas_call(
        _mod_kernel,
        grid=(depth, n),
        in_specs=[pl.BlockSpec((bsz, d), lambda l, j: (0, 0)),
                  pl.BlockSpec((1, d, d), lambda l, j: (l, 0, j)),
                  pl.BlockSpec((1, 1, 1, d), lambda l, j: (l, j, 0, 0))],
        out_specs=pl.BlockSpec((1, 1, bsz, d), lambda l, j: (l, j, 0, 0)),
        out_shape=jax.ShapeDtypeStruct((depth, n, bsz, d), F32),
        compiler_params=_params("arbitrary", "arbitrary"),
        name="adaln_mod",
    )(c, w_ada, b_ada.reshape(depth, n, 1, d))


def _in_kernel(x_ref, mod_ref, g_ref, w_ref, wvf_ref, u_ref, q_ref, k_ref, vt_ref, f_ref,
               *, ssm_w, attn_w):
    b = pl.program_id(0)
    x = x_ref[0]
    h = _rmsnorm(x, g_ref[...]) * (1.0 + _mod_row(mod_ref, 1, b)) + _mod_row(mod_ref, 0, b)
    hb = h.astype(BF16)
    proj = jnp.dot(hb, w_ref[...], preferred_element_type=F32)
    u_ref[0] = proj[:, :ssm_w]
    scale = HEAD_DIM ** -0.5 * LOG2E
    proj_t = lax.dot_general(wvf_ref[...], hb, (((1,), (1,)), ((), ())),
                             preferred_element_type=F32)
    lane = lax.broadcasted_iota(jnp.int32, (x.shape[0], LANES), 1)
    for p in range(attn_w // LANES):
        c0 = ssm_w + p * LANES
        q_pair = proj[:, c0:c0 + LANES] * scale
        q_ref[0, p, 0] = jnp.where(lane < HEAD_DIM, q_pair, 0.0).astype(BF16)
        q_ref[0, p, 1] = jnp.where(lane >= HEAD_DIM, q_pair, 0.0).astype(BF16)
        k_ref[0, p] = proj[:, c0 + attn_w:c0 + attn_w + LANES].astype(BF16)
        vt_ref[0, p] = proj_t[p * LANES:(p + 1) * LANES, :].astype(BF16)
    f_ref[0] = proj_t[attn_w:, :]


def _in_projection(x, mod, g, w_main, w_vf_t, ssm_w, attn_w):
    bsz, s, d = x.shape
    tm = min(TOKEN_TILE, s)
    pairs = attn_w // LANES
    heads = w_vf_t.shape[0] - attn_w
    kern = functools.partial(_in_kernel, ssm_w=ssm_w, attn_w=attn_w)
    qk_shape = jax.ShapeDtypeStruct((bsz, pairs, s, LANES), BF16)
    qk_spec = pl.BlockSpec((1, pairs, tm, LANES), lambda b, i: (b, 0, i, 0))
    return pl.pallas_call(
        kern,
        grid=(bsz, s // tm),
        in_specs=[pl.BlockSpec((1, tm, d), lambda b, i: (b, i, 0)),
                  pl.BlockSpec(mod.shape, lambda b, i: (0, 0, 0)),
                  pl.BlockSpec((1, d), lambda b, i: (0, 0)),
                  pl.BlockSpec(w_main.shape, lambda b, i: (0, 0)),
                  pl.BlockSpec(w_vf_t.shape, lambda b, i: (0, 0))],
        out_specs=[pl.BlockSpec((1, tm, ssm_w), lambda b, i: (b, i, 0)),
                   pl.BlockSpec((1, pairs, 2, tm, LANES), lambda b, i: (b, 0, 0, i, 0)),
                   qk_spec,
                   pl.BlockSpec((1, pairs, LANES, tm), lambda b, i: (b, 0, 0, i)),
                   pl.BlockSpec((1, heads, tm), lambda b, i: (b, 0, i))],
        out_shape=[jax.ShapeDtypeStruct((bsz, s, ssm_w), F32),
                   jax.ShapeDtypeStruct((bsz, pairs, 2, s, LANES), BF16),
                   qk_shape,
                   jax.ShapeDtypeStruct((bsz, pairs, LANES, s), BF16),
                   jax.ShapeDtypeStruct((bsz, heads, s), F32)],
        compiler_params=_params("arbitrary", "arbitrary"),
        name="in_projection",
    )(x, mod, g.reshape(1, d), w_main, w_vf_t)


def _cum_kernel(f_ref, bf_ref, o_ref):
    x = jax.nn.log_sigmoid(f_ref[0] + bf_ref[...])
    s = x.shape[1]
    lane = lax.broadcasted_iota(jnp.int32, x.shape, 1)
    shift = 1
    while shift < s:
        x = x + jnp.where(lane >= shift, pltpu.roll(x, shift, axis=1), 0.0)
        shift *= 2
    o_ref[0] = x * LOG2E


def _forget_cumsum(f_t, b_f):
    bsz, heads, s = f_t.shape
    return pl.pallas_call(
        _cum_kernel,
        grid=(bsz,),
        in_specs=[pl.BlockSpec((1, heads, s), lambda b: (b, 0, 0)),
                  pl.BlockSpec((heads, 1), lambda b: (0, 0))],
        out_specs=pl.BlockSpec((1, heads, s), lambda b: (b, 0, 0)),
        out_shape=jax.ShapeDtypeStruct((bsz, heads, s), F32),
        compiler_params=_params("arbitrary"),
        name="forget_cumsum",
    )(f_t, b_f.reshape(heads, 1))


def _ssm_kernel(u_ref, bm_ref, are_ref, aim_ref, cm_ref, dsk_ref, wg_ref, bg_ref, o_ref,
                buf, st, *, steps, n_slabs):
    @pl.when(pl.program_id(0) == 0)
    def _():
        st[...] = jnp.zeros_like(st)

    bsz, _, width = u_ref.shape
    rows = steps * bsz
    u = jnp.swapaxes(u_ref[...], 0, 1).reshape(rows, width)
    ub = u.astype(BF16)
    slabs_per_block = LANES // (2 * SSM_GROUP_CH)
    for j in range(n_slabs):
        m = j // slabs_per_block
        buf[j] = jnp.dot(ub[:, m * LANES:(m + 1) * LANES], bm_ref[j],
                         preferred_element_type=F32)

    for first in range(0, n_slabs, SCAN_GROUP):
        slabs = list(range(first, first + SCAN_GROUP))
        coefs = [(are_ref[j], aim_ref[j]) for j in slabs]
        init = tuple((st[j, :, :LANES], st[j, :, LANES:]) for j in slabs)

        def body(t, carry, slabs=slabs, coefs=coefs):
            row = pl.multiple_of(t * SUBLANES, SUBLANES)
            new = []
            for (re, im), (are, aim), j in zip(carry, coefs, slabs):
                bre = buf[j, pl.ds(row, SUBLANES), :LANES]
                bim = buf[j, pl.ds(row, SUBLANES), LANES:]
                nre = are * re - aim * im + bre
                nim = are * im + aim * re + bim
                buf[j, pl.ds(row, SUBLANES), :LANES] = nre
                buf[j, pl.ds(row, SUBLANES), LANES:] = nim
                new.append((nre, nim))
            return tuple(new)

        fin = lax.fori_loop(0, steps, body, init, unroll=SCAN_UNROLL)
        for (re, im), j in zip(fin, slabs):
            st[j, :, :LANES] = re
            st[j, :, LANES:] = im

    ys = []
    for m in range(n_slabs // slabs_per_block):
        acc = None
        for j in range(m * slabs_per_block, (m + 1) * slabs_per_block):
            part = jnp.dot(buf[j].astype(BF16), cm_ref[j], preferred_element_type=F32)
            acc = part if acc is None else acc + part
        ys.append(acc)
    y = jnp.concatenate(ys, axis=1) + dsk_ref[...] * u
    z = jax.nn.gelu(y, approximate=True)
    gate = jax.nn.sigmoid(
        jnp.dot(z.astype(BF16), wg_ref[...], preferred_element_type=F32) + bg_ref[...])
    out = (z * gate).reshape(steps, bsz, width)
    o_ref[...] = jnp.swapaxes(out, 0, 1).astype(BF16)


def _ssm_branch(u, bmat, a_re, a_im, cmat, d_skip, w_glu, b_glu):
    bsz, s, width = u.shape
    steps = min(SCAN_STEPS, s)
    rows = steps * bsz
    n_slabs = bmat.shape[0]
    kern = functools.partial(_ssm_kernel, steps=steps, n_slabs=n_slabs)
    full = lambda a: pl.BlockSpec(a.shape, lambda i: (0,) * a.ndim)
    d_skip = d_skip.reshape(1, width)
    b_glu = b_glu.reshape(1, width)
    return pl.pallas_call(
        kern,
        grid=(s // steps,),
        in_specs=[pl.BlockSpec((bsz, steps, width), lambda i: (0, i, 0)),
                  full(bmat), full(a_re), full(a_im), full(cmat),
                  full(d_skip), full(w_glu), full(b_glu)],
        out_specs=pl.BlockSpec((bsz, steps, width), lambda i: (0, i, 0)),
        out_shape=jax.ShapeDtypeStruct((bsz, s, width), BF16),
        scratch_shapes=[pltpu.VMEM((n_slabs, rows, SLAB), F32),
                        pltpu.VMEM((n_slabs, bsz, SLAB), F32)],
        compiler_params=_params("arbitrary"),
        name="s5_branch",
    )(u, bmat, a_re, a_im, cmat, d_skip, w_glu, b_glu)


def _s5_matrices(lam_re, lam_im, log_dt, b_re, b_im, c_re, c_im, bsz):
    g, p = lam_re.shape
    h = b_re.shape[-1]
    lr = jnp.minimum(lam_re, -EIG_CLIP)
    li = lam_im
    dt = jnp.exp(log_dt)[:, None]
    mag = jnp.exp(lr * dt)
    a_re = mag * jnp.cos(li * dt)
    a_im = mag * jnp.sin(li * dt)
    den = lr * lr + li * li
    f_re = ((a_re - 1.0) * lr + a_im * li) / den
    f_im = (a_im * lr - (a_re - 1.0) * li) / den
    bb_re = f_re[..., None] * b_re - f_im[..., None] * b_im
    bb_im = f_re[..., None] * b_im + f_im[..., None] * b_re
    n_slabs = g // 2
    per_block = LANES // (2 * h)
    n_blocks = n_slabs // per_block
    eye_s = jnp.eye(per_block, dtype=F32)
    eye_g = jnp.eye(2, dtype=F32)

    def expand(w_hp, spec):
        w = w_hp.reshape(n_blocks, per_block, 2, h, p)
        return jnp.einsum(spec, w, eye_s, eye_g)

    b_spec = "mjghp,ja,gk->mjaghkp"
    bmat = jnp.concatenate(
        [expand(bb.transpose(0, 2, 1), b_spec).reshape(n_slabs, LANES, LANES)
         for bb in (bb_re, bb_im)], axis=2)
    c_spec = "mjghp,ja,gk->mjkpagh"
    cmat = jnp.concatenate(
        [expand(cc, c_spec).reshape(n_slabs, LANES, LANES) for cc in (c_re, -c_im)], axis=1)
    a_re_s = a_re.reshape(n_slabs, 1, 2 * p)
    a_im_s = a_im.reshape(n_slabs, 1, 2 * p)
    a_re_s = jnp.broadcast_to(a_re_s, (n_slabs, bsz, 2 * p))
    a_im_s = jnp.broadcast_to(a_im_s, (n_slabs, bsz, 2 * p))
    return bmat.astype(BF16), a_re_s, a_im_s, cmat.astype(BF16)


def _attn_kernel(ti_ref, tj_ref, q_ref, k_ref, vt_ref, ccol_ref, o_ref,
                 s_buf, p_buf, a_buf, m_sc, l_sc, acc_sc, *, tq, tk, n_diag, n_steps):
    m_sc[...] = jnp.full_like(m_sc, -jnp.inf)
    l_sc[...] = jnp.zeros_like(l_sc)
    acc_sc[...] = jnp.zeros_like(acc_sc)
    p_buf[1] = jnp.zeros(p_buf.shape[1:], BF16)
    a_buf[1] = jnp.ones(a_buf.shape[1:], F32)

    def tile_of(n):
        i = ti_ref[n]
        return i, pl.multiple_of(i * tq, tq), pl.multiple_of(tj_ref[n] * tk, tk)

    def scores(n, slot):
        _, qs, ks = tile_of(n)
        kt = k_ref[0, 0, pl.ds(ks, tk), :]
        for hh in range(2):
            s_buf[slot, hh] = lax.dot_general(
                kt, q_ref[0, 0, hh, pl.ds(qs, tq), :], (((1,), (1,)), ((), ())),
                preferred_element_type=F32)

    def softmax(n, slot, masked):
        i, qs, ks = tile_of(n)
        for hh in range(2):
            ck = ccol_ref[0, 0, pl.ds(ks, tk), hh:hh + 1]
            s = s_buf[slot, hh] - ck
            if masked:
                kpos = ks + lax.broadcasted_iota(jnp.int32, s.shape, 0)
                qpos = qs + lax.broadcasted_iota(jnp.int32, s.shape, 1)
                s = jnp.where(kpos <= qpos, s, -jnp.inf)
            m_old = m_sc[i, hh]
            m_new = jnp.maximum(m_old, jnp.max(s, axis=0, keepdims=True))
            alpha = jnp.exp2(m_old - m_new)
            pexp = jnp.exp2(s - m_new)
            l_sc[i, hh] = alpha * l_sc[i, hh] + jnp.sum(pexp, axis=0, keepdims=True)
            m_sc[i, hh] = m_new
            a_buf[slot, hh] = alpha
            p_buf[slot, hh] = pexp.astype(BF16)

    def values(n, slot):
        i, _, ks = tile_of(n)
        vt = vt_ref[0, 0, :, pl.ds(ks, tk)]
        for hh in range(2):
            rows = slice(hh * HEAD_DIM, (hh + 1) * HEAD_DIM)
            acc_sc[i, rows, :] = a_buf[slot, hh] * acc_sc[i, rows, :] + jnp.dot(
                vt[rows, :], p_buf[slot, hh], preferred_element_type=F32)

    def step_pair(masked):
        def body(t, carry):
            n = 2 * t
            scores(n + 1, 1)
            softmax(n, 0, masked)
            values(jnp.maximum(n - 1, 0), 1)
            scores(n + 2, 0)
            softmax(n + 1, 1, masked)
            values(n, 0)
            return carry
        return body

    scores(0, 0)
    lax.fori_loop(0, n_diag // 2, step_pair(True), 0)
    lax.fori_loop(n_diag // 2, n_steps // 2, step_pair(False), 0)
    values(n_steps - 1, 1)

    def finalize(i, carry):
        out_t = jnp.concatenate([acc_sc[i, :HEAD_DIM, :] / l_sc[i, 0],
                                 acc_sc[i, HEAD_DIM:, :] / l_sc[i, 1]], axis=0)
        o_ref[0, pl.ds(pl.multiple_of(i * tq, tq), tq), :] = out_t.T.astype(BF16)
        return carry

    lax.fori_loop(0, n_diag, finalize, 0)


def _attention_steps(s, tq, tk):
    nq = s // tq
    diag = [(i, i * tq // tk) for i in range(nq)]
    full = [(i, j) for i in range(nq) for j in range(i * tq // tk)]
    steps = diag + full
    assert nq % 2 == 0 and len(steps) % 2 == 0, "the pipeline advances two steps per iteration"
    padded = steps + [steps[-1]] * 2
    ti = jnp.asarray([i for i, _ in padded], jnp.int32)
    tj = jnp.asarray([j for _, j in padded], jnp.int32)
    return ti, tj, nq, len(steps)


def _attention(q, k, v_t, cum_col):
    bsz, pairs, _, s, _ = q.shape
    tq = min(ATTN_Q_TILE, s)
    tk = min(ATTN_K_TILE, s)
    assert tk % tq == 0
    ti, tj, nq, n_steps = _attention_steps(s, tq, tk)
    kern = functools.partial(_attn_kernel, tq=tq, tk=tk, n_diag=nq, n_steps=n_steps)
    grid_spec = pltpu.PrefetchScalarGridSpec(
        num_scalar_prefetch=2,
        grid=(bsz, pairs),
        in_specs=[pl.BlockSpec((1, 1, 2, s, LANES), lambda b, p, ti, tj: (b, p, 0, 0, 0)),
                  pl.BlockSpec((1, 1, s, LANES), lambda b, p, ti, tj: (b, p, 0, 0)),
                  pl.BlockSpec((1, 1, LANES, s), lambda b, p, ti, tj: (b, p, 0, 0)),
                  pl.BlockSpec((1, 1, s, 2), lambda b, p, ti, tj: (b, p, 0, 0))],
        out_specs=pl.BlockSpec((1, s, LANES), lambda b, p, ti, tj: (b, 0, p)),
        scratch_shapes=[pltpu.VMEM((2, 2, tk, tq), F32),
                        pltpu.VMEM((2, 2, tk, tq), BF16),
                        pltpu.VMEM((2, 2, 1, tq), F32),
                        pltpu.VMEM((nq, 2, 1, tq), F32),
                        pltpu.VMEM((nq, 2, 1, tq), F32),
                        pltpu.VMEM((nq, LANES, tq), F32)])
    return pl.pallas_call(
        kern,
        grid_spec=grid_spec,
        out_shape=jax.ShapeDtypeStruct((bsz, s, pairs * LANES), BF16),
        compiler_params=_params("arbitrary", "arbitrary"),
        name="forgetting_attention",
    )(ti, tj, q, k, v_t, cum_col)


def _merge_kernel(x_ref, mod_ref, gpre_ref, gpost_ref, wgate_ref, ys_ref, ya_ref,
                  wpa_ref, wpb_ref, wo_ref, o_ref):
    b = pl.program_id(0)
    x = x_ref[0]
    d = x.shape[-1]
    h = _rmsnorm(x, gpre_ref[...]) * (1.0 + _mod_row(mod_ref, 1, b)) + _mod_row(mod_ref, 0, b)
    gates = jnp.dot(h.astype(BF16), wgate_ref[...], preferred_element_type=F32)
    pa = jnp.dot(ys_ref[0], wpa_ref[...], preferred_element_type=F32)
    pb = jnp.dot(ya_ref[0], wpb_ref[...], preferred_element_type=F32)
    merged = jax.nn.sigmoid(gates[:, :d]) * pa + jax.nn.sigmoid(gates[:, d:]) * pb
    y = jnp.dot(merged.astype(BF16), wo_ref[...], preferred_element_type=F32)
    o_ref[0] = x + _mod_row(mod_ref, 2, b) * _rmsnorm(y, gpost_ref[...])


def _merge(x, mod, g_pre, g_post, w_gate, y_ssm, y_att, w_pa, w_pb, w_o):
    bsz, s, d = x.shape
    tm = min(TOKEN_TILE, s)
    width = y_att.shape[-1]
    full = lambda a: pl.BlockSpec(a.shape, lambda b, i: (0,) * a.ndim)
    g_pre = g_pre.reshape(1, d)
    g_post = g_post.reshape(1, d)
    return pl.pallas_call(
        _merge_kernel,
        grid=(bsz, s // tm),
        in_specs=[pl.BlockSpec((1, tm, d), lambda b, i: (b, i, 0)),
                  full(mod), full(g_pre), full(g_post), full(w_gate),
                  pl.BlockSpec((1, tm, width), lambda b, i: (b, i, 0)),
                  pl.BlockSpec((1, tm, width), lambda b, i: (b, i, 0)),
                  full(w_pa), full(w_pb), full(w_o)],
        out_specs=pl.BlockSpec((1, tm, d), lambda b, i: (b, i, 0)),
        out_shape=jax.ShapeDtypeStruct(x.shape, F32),
        compiler_params=_params("arbitrary", "arbitrary"),
        name="gated_merge",
    )(x, mod, g_pre, g_post, w_gate, y_ssm, y_att, w_pa, w_pb, w_o)


def _ffn_chunks(d_ff):
    mxu = 2 * LANES
    n_tiles = d_ff // mxu
    first = (n_tiles + 1) // 2 * mxu
    return ((0, first), (first, d_ff)) if first < d_ff else ((0, d_ff),)


def _ffn_kernel(x_ref, mod_ref, gpre_ref, gpost_ref, wg_ref, wu_ref, wd_ref, o_ref, *, chunks):
    b = pl.program_id(0)
    x = x_ref[0]
    h = _rmsnorm(x, gpre_ref[...]) * (1.0 + _mod_row(mod_ref, 4, b)) + _mod_row(mod_ref, 3, b)
    hb = h.astype(BF16)
    acc = None
    for c0, c1 in chunks:
        g = jnp.dot(hb, wg_ref[:, c0:c1], preferred_element_type=F32)
        up = jnp.dot(hb, wu_ref[:, c0:c1], preferred_element_type=F32)
        a = (g * jax.nn.sigmoid(g) * up).astype(BF16)
        part = jnp.dot(a, wd_ref[c0:c1, :], preferred_element_type=F32)
        acc = part if acc is None else acc + part
    o_ref[0] = x + _mod_row(mod_ref, 5, b) * _rmsnorm(acc, gpost_ref[...])


def _ffn(x, mod, g_pre, g_post, w_gate, w_up, w_down):
    bsz, s, d = x.shape
    tm = min(TOKEN_TILE, s)
    kern = functools.partial(_ffn_kernel, chunks=_ffn_chunks(w_gate.shape[1]))
    full = lambda a: pl.BlockSpec(a.shape, lambda b, i: (0,) * a.ndim)
    g_pre = g_pre.reshape(1, d)
    g_post = g_post.reshape(1, d)
    return pl.pallas_call(
        kern,
        grid=(bsz, s // tm),
        in_specs=[pl.BlockSpec((1, tm, d), lambda b, i: (b, i, 0)),
                  full(mod), full(g_pre), full(g_post),
                  full(w_gate), full(w_up), full(w_down)],
        out_specs=pl.BlockSpec((1, tm, d), lambda b, i: (b, i, 0)),
        out_shape=jax.ShapeDtypeStruct(x.shape, F32),
        compiler_params=_params("arbitrary", "arbitrary"),
        name="swiglu_ffn",
    )(x, mod, g_pre, g_post, w_gate, w_up, w_down)


def kernel(x, c, w_ada, b_ada, g_pre_mix, g_post_mix, g_pre_ffn, g_post_ffn, w_in, lam_re, lam_im, log_dt, b_re, b_im, c_re, c_im, d_skip, w_glu, b_glu, b_f, w_pa, w_pb, w_o, w_ffn_gate, w_ffn_up, w_ffn_down):
    bsz, s, d = x.shape
    depth = w_in.shape[0]
    ssm_w = d_skip.shape[1]
    heads = b_f.shape[1]
    attn_w = heads * HEAD_DIM
    pairs = attn_w // LANES
    main_w = ssm_w + 3 * attn_w
    assert bsz == SUBLANES, "the S5 scan keeps the batch on the sublane axis"

    mod_all = _modulation(c, w_ada, b_ada)
    for l in range(depth):
        mod = mod_all[l]
        uqk_w = ssm_w + 2 * attn_w
        w_main = w_in[l, :, :uqk_w].astype(BF16)
        w_vf_t = w_in[l, :, uqk_w:main_w + heads].T.astype(BF16)
        w_gate = w_in[l, :, main_w + heads:].astype(BF16)

        u, q, k, v_t, f_t = _in_projection(x, mod, g_pre_mix[l], w_main, w_vf_t, ssm_w, attn_w)

        cum = _forget_cumsum(f_t, b_f[l])
        cum_col = cum.reshape(bsz, pairs, 2, s).transpose(0, 1, 3, 2)
        y_att = _attention(q, k, v_t, cum_col)

        bmat, a_re, a_im, cmat = _s5_matrices(lam_re[l], lam_im[l], log_dt[l], b_re[l], b_im[l],
                                              c_re[l], c_im[l], bsz)
        y_ssm = _ssm_branch(u, bmat, a_re, a_im, cmat,
                            d_skip[l], w_glu[l].astype(BF16), b_glu[l])

        x = _merge(x, mod, g_pre_mix[l], g_post_mix[l], w_gate, y_ssm, y_att,
                   w_pa[l].astype(BF16), w_pb[l].astype(BF16), w_o[l].astype(BF16))
        x = _ffn(x, mod, g_pre_ffn[l], g_post_ffn[l], w_ffn_gate[l].astype(BF16),
                 w_ffn_up[l].astype(BF16), w_ffn_down[l].astype(BF16))
    return x
```

```python
import functools
import math

import jax
import jax.numpy as jnp
from jax import lax
from jax.experimental import pallas as pl
from jax.experimental.pallas import tpu as pltpu

F32 = jnp.float32
BF16 = jnp.bfloat16

SSM_GROUP_CH = 16
SSM_STATE = 64
HEAD_DIM = 64
N_MOD = 6
RMS_EPS = 1e-6
EIG_CLIP = 1e-4
LOG2E = math.log2(math.e)

LANES = 128
SUBLANES = 8
SLAB = 2 * LANES
VMEM_LIMIT = 56 * 1024 * 1024

TOKEN_TILE = 512
SCAN_STEPS = 64
ATTN_Q_TILE = 256
ATTN_K_TILE = 512
SCAN_GROUP = 4
SCAN_UNROLL = 4


def _params(*sem):
    return pltpu.CompilerParams(dimension_semantics=sem, vmem_limit_bytes=VMEM_LIMIT)


def _rmsnorm(x, g):
    r = lax.rsqrt(jnp.mean(x * x, axis=-1, keepdims=True) + RMS_EPS)
    return x * r * g


def _mod_row(mod_ref, j, b):
    return mod_ref[j, pl.ds(b, 1), :]


def _mod_kernel(c_ref, w_ref, b_ref, o_ref):
    c = c_ref[...]
    cond = c * jax.nn.sigmoid(c)
    o_ref[0, 0] = jnp.dot(cond.astype(BF16), w_ref[0].astype(BF16),
                          preferred_element_type=F32) + b_ref[0, 0]


def _modulation(c, w_ada, b_ada):
    depth, d, nd = w_ada.shape
    bsz = c.shape[0]
    n = nd // d
    return pl.pallas_call(
        _mod_kernel,
        grid=(depth, n),
        in_specs=[pl.BlockSpec((bsz, d), lambda l, j: (0, 0)),
                  pl.BlockSpec((1, d, d), lambda l, j: (l, 0, j)),
                  pl.BlockSpec((1, 1, 1, d), lambda l, j: (l, j, 0, 0))],
        out_specs=pl.BlockSpec((1, 1, bsz, d), lambda l, j: (l, j, 0, 0)),
        out_shape=jax.ShapeDtypeStruct((depth, n, bsz, d), F32),
        compiler_params=_params("arbitrary", "arbitrary"),
        name="adaln_mod",
    )(c, w_ada, b_ada.reshape(depth, n, 1, d))


def _in_kernel(x_ref, mod_ref, g_ref, w_ref, wvf_ref, u_ref, q_ref, k_ref, vt_ref, f_ref,
               *, ssm_w, attn_w):
    b = pl.program_id(0)
    x = x_ref[0]
    h = _rmsnorm(x, g_ref[...]) * (1.0 + _mod_row(mod_ref, 1, b)) + _mod_row(mod_ref, 0, b)
    hb = h.astype(BF16)
    proj = jnp.dot(hb, w_ref[...], preferred_element_type=F32)
    u_ref[0] = proj[:, :ssm_w]
    scale = HEAD_DIM ** -0.5 * LOG2E
    proj_t = lax.dot_general(wvf_ref[...], hb, (((1,), (1,)), ((), ())),
                             preferred_element_type=F32)
    lane = lax.broadcasted_iota(jnp.int32, (x.shape[0], LANES), 1)
    for p in range(attn_w // LANES):
        c0 = ssm_w + p * LANES
        q_pair = proj[:, c0:c0 + LANES] * scale
        q_ref[0, p, 0] = jnp.where(lane < HEAD_DIM, q_pair, 0.0).astype(BF16)
        q_ref[0, p, 1] = jnp.where(lane >= HEAD_DIM, q_pair, 0.0).astype(BF16)
        k_ref[0, p] = proj[:, c0 + attn_w:c0 + attn_w + LANES].astype(BF16)
        vt_ref[0, p] = proj_t[p * LANES:(p + 1) * LANES, :].astype(BF16)
    f_ref[0] = proj_t[attn_w:, :]


def _in_projection(x, mod, g, w_main, w_vf_t, ssm_w, attn_w):
    bsz, s, d = x.shape
    tm = min(TOKEN_TILE, s)
    pairs = attn_w // LANES
    heads = w_vf_t.shape[0] - attn_w
    kern = functools.partial(_in_kernel, ssm_w=ssm_w, attn_w=attn_w)
    qk_shape = jax.ShapeDtypeStruct((bsz, pairs, s, LANES), BF16)
    qk_spec = pl.BlockSpec((1, pairs, tm, LANES), lambda b, i: (b, 0, i, 0))
    return pl.pallas_call(
        kern,
        grid=(bsz, s // tm),
        in_specs=[pl.BlockSpec((1, tm, d), lambda b, i: (b, i, 0)),
                  pl.BlockSpec(mod.shape, lambda b, i: (0, 0, 0)),
                  pl.BlockSpec((1, d), lambda b, i: (0, 0)),
                  pl.BlockSpec(w_main.shape, lambda b, i: (0, 0)),
                  pl.BlockSpec(w_vf_t.shape, lambda b, i: (0, 0))],
        out_specs=[pl.BlockSpec((1, tm, ssm_w), lambda b, i: (b, i, 0)),
                   pl.BlockSpec((1, pairs, 2, tm, LANES), lambda b, i: (b, 0, 0, i, 0)),
                   qk_spec,
                   pl.BlockSpec((1, pairs, LANES, tm), lambda b, i: (b, 0, 0, i)),
                   pl.BlockSpec((1, heads, tm), lambda b, i: (b, 0, i))],
        out_shape=[jax.ShapeDtypeStruct((bsz, s, ssm_w), F32),
                   jax.ShapeDtypeStruct((bsz, pairs, 2, s, LANES), BF16),
                   qk_shape,
                   jax.ShapeDtypeStruct((bsz, pairs, LANES, s), BF16),
                   jax.ShapeDtypeStruct((bsz, heads, s), F32)],
        compiler_params=_params("arbitrary", "arbitrary"),
        name="in_projection",
    )(x, mod, g.reshape(1, d), w_main, w_vf_t)


def _cum_kernel(f_ref, bf_ref, o_ref):
    x = jax.nn.log_sigmoid(f_ref[0] + bf_ref[...])
    s = x.shape[1]
    lane = lax.broadcasted_iota(jnp.int32, x.shape, 1)
    shift = 1
    while shift < s:
        x = x + jnp.where(lane >= shift, pltpu.roll(x, shift, axis=1), 0.0)
        shift *= 2
    cum_t = (x * LOG2E).T
    for p in range(o_ref.shape[1]):
        o_ref[0, p] = cum_t[:, 2 * p:2 * p + 2]


def _forget_cumsum(f_t, b_f):
    bsz, heads, s = f_t.shape
    pairs = heads // 2
    return pl.pallas_call(
        _cum_kernel,
        grid=(bsz,),
        in_specs=[pl.BlockSpec((1, heads, s), lambda b: (b, 0, 0)),
                  pl.BlockSpec((heads, 1), lambda b: (0, 0))],
        out_specs=pl.BlockSpec((1, pairs, s, 2), lambda b: (b, 0, 0, 0)),
        out_shape=jax.ShapeDtypeStruct((bsz, pairs, s, 2), F32),
        compiler_params=_params("arbitrary"),
        name="forget_cumsum",
    )(f_t, b_f.reshape(heads, 1))


def _ssm_kernel(u_ref, bm_ref, are_ref, aim_ref, cm_ref, dsk_ref, wg_ref, bg_ref, o_ref,
                buf, st, *, steps, n_slabs):
    @pl.when(pl.program_id(0) == 0)
    def _():
        st[...] = jnp.zeros_like(st)

    bsz, _, width = u_ref.shape
    rows = steps * bsz
    u = jnp.swapaxes(u_ref[...], 0, 1).reshape(rows, width)
    ub = u.astype(BF16)
    slabs_per_block = LANES // (2 * SSM_GROUP_CH)
    for j in range(n_slabs):
        m = j // slabs_per_block
        buf[j] = jnp.dot(ub[:, m * LANES:(m + 1) * LANES], bm_ref[j],
                         preferred_element_type=F32)

    for first in range(0, n_slabs, SCAN_GROUP):
        slabs = list(range(first, first + SCAN_GROUP))
        coefs = [(are_ref[j], aim_ref[j]) for j in slabs]
        init = tuple((st[j, :, :LANES], st[j, :, LANES:]) for j in slabs)

        def body(t, carry, slabs=slabs, coefs=coefs):
            row = t * SUBLANES
            new = []
            for (re, im), (are, aim), j in zip(carry, coefs, slabs):
                bre = buf[j, pl.ds(row, SUBLANES), :LANES]
                bim = buf[j, pl.ds(row, SUBLANES), LANES:]
                nre = are * re - aim * im + bre
                nim = are * im + aim * re + bim
                buf[j, pl.ds(row, SUBLANES), :LANES] = nre
                buf[j, pl.ds(row, SUBLANES), LANES:] = nim
                new.append((nre, nim))
            return tuple(new)

        fin = init
        for t in range(steps):
            fin = body(t, fin)
        for (re, im), j in zip(fin, slabs):
            st[j, :, :LANES] = re
            st[j, :, LANES:] = im

    ys = []
    for m in range(n_slabs // slabs_per_block):
        acc = None
        for j in range(m * slabs_per_block, (m + 1) * slabs_per_block):
            part = jnp.dot(buf[j].astype(BF16), cm_ref[j], preferred_element_type=F32)
            acc = part if acc is None else acc + part
        ys.append(acc)
    y = jnp.concatenate(ys, axis=1) + dsk_ref[...] * u
    z = jax.nn.gelu(y, approximate=True)
    gate = jax.nn.sigmoid(
        jnp.dot(z.astype(BF16), wg_ref[...], preferred_element_type=F32) + bg_ref[...])
    out = (z * gate).reshape(steps, bsz, width)
    o_ref[...] = jnp.swapaxes(out, 0, 1).astype(BF16)


def _ssm_branch(u, bmat, a_re, a_im, cmat, d_skip, w_glu, b_glu):
    bsz, s, width = u.shape
    steps = min(SCAN_STEPS, s)
    rows = steps * bsz
    n_slabs = bmat.shape[0]
    kern = functools.partial(_ssm_kernel, steps=steps, n_slabs=n_slabs)
    full = lambda a: pl.BlockSpec(a.shape, lambda i: (0,) * a.ndim)
    d_skip = d_skip.reshape(1, width)
    b_glu = b_glu.reshape(1, width)
    return pl.pallas_call(
        kern,
        grid=(s // steps,),
        in_specs=[pl.BlockSpec((bsz, steps, width), lambda i: (0, i, 0)),
                  full(bmat), full(a_re), full(a_im), full(cmat),
                  full(d_skip), full(w_glu), full(b_glu)],
        out_specs=pl.BlockSpec((bsz, steps, width), lambda i: (0, i, 0)),
        out_shape=jax.ShapeDtypeStruct((bsz, s, width), BF16),
        scratch_shapes=[pltpu.VMEM((n_slabs, rows, SLAB), F32),
                        pltpu.VMEM((n_slabs, bsz, SLAB), F32)],
        compiler_params=_params("arbitrary"),
        name="s5_branch",
    )(u, bmat, a_re, a_im, cmat, d_skip, w_glu, b_glu)


def _s5_matrices(lam_re, lam_im, log_dt, b_re, b_im, c_re, c_im, bsz):
    g, p = lam_re.shape
    h = b_re.shape[-1]
    lr = jnp.minimum(lam_re, -EIG_CLIP)
    li = lam_im
    dt = jnp.exp(log_dt)[:, None]
    mag = jnp.exp(lr * dt)
    a_re = mag * jnp.cos(li * dt)
    a_im = mag * jnp.sin(li * dt)
    den = lr * lr + li * li
    f_re = ((a_re - 1.0) * lr + a_im * li) / den
    f_im = (a_im * lr - (a_re - 1.0) * li) / den
    bb_re = f_re[..., None] * b_re - f_im[..., None] * b_im
    bb_im = f_re[..., None] * b_im + f_im[..., None] * b_re
    n_slabs = g // 2
    per_block = LANES // (2 * h)
    n_blocks = n_slabs // per_block
    eye_s = jnp.eye(per_block, dtype=F32)
    eye_g = jnp.eye(2, dtype=F32)

    def expand(w_hp, spec):
        w = w_hp.reshape(n_blocks, per_block, 2, h, p)
        return jnp.einsum(spec, w, eye_s, eye_g)

    b_spec = "mjghp,ja,gk->mjaghkp"
    bmat = jnp.concatenate(
        [expand(bb.transpose(0, 2, 1), b_spec).reshape(n_slabs, LANES, LANES)
         for bb in (bb_re, bb_im)], axis=2)
    c_spec = "mjghp,ja,gk->mjkpagh"
    cmat = jnp.concatenate(
        [expand(cc, c_spec).reshape(n_slabs, LANES, LANES) for cc in (c_re, -c_im)], axis=1)
    a_re_s = a_re.reshape(n_slabs, 1, 2 * p)
    a_im_s = a_im.reshape(n_slabs, 1, 2 * p)
    a_re_s = jnp.broadcast_to(a_re_s, (n_slabs, bsz, 2 * p))
    a_im_s = jnp.broadcast_to(a_im_s, (n_slabs, bsz, 2 * p))
    return bmat.astype(BF16), a_re_s, a_im_s, cmat.astype(BF16)


def _attn_kernel(ti_ref, tj_ref, q_ref, k_ref, vt_ref, ccol_ref, o_ref,
                 s_buf, p_buf, a_buf, m_sc, l_sc, acc_sc, *, tq, tk, n_diag, n_steps):
    m_sc[...] = jnp.full_like(m_sc, -jnp.inf)
    l_sc[...] = jnp.zeros_like(l_sc)
    acc_sc[...] = jnp.zeros_like(acc_sc)
    p_buf[1] = jnp.zeros(p_buf.shape[1:], BF16)
    a_buf[1] = jnp.ones(a_buf.shape[1:], F32)

    def tile_of(n):
        i = ti_ref[n]
        return i, pl.multiple_of(i * tq, tq), pl.multiple_of(tj_ref[n] * tk, tk)

    def scores(n, slot):
        _, qs, ks = tile_of(n)
        kt = k_ref[0, 0, pl.ds(ks, tk), :]
        for hh in range(2):
            s_buf[slot, hh] = lax.dot_general(
                kt, q_ref[0, 0, hh, pl.ds(qs, tq), :], (((1,), (1,)), ((), ())),
                preferred_element_type=F32)

    def softmax(n, slot, masked):
        i, qs, ks = tile_of(n)
        for hh in range(2):
            ck = ccol_ref[0, 0, pl.ds(ks, tk), hh:hh + 1]
            s = s_buf[slot, hh] - ck
            if masked:
                kpos = ks + lax.broadcasted_iota(jnp.int32, s.shape, 0)
                qpos = qs + lax.broadcasted_iota(jnp.int32, s.shape, 1)
                s = jnp.where(kpos <= qpos, s, -jnp.inf)
            m_old = m_sc[i, hh]
            m_new = jnp.maximum(m_old, jnp.max(s, axis=0, keepdims=True))
            alpha = jnp.exp2(m_old - m_new)
            pexp = jnp.exp2(s - m_new)
            l_sc[i, hh] = alpha * l_sc[i, hh] + jnp.sum(pexp, axis=0, keepdims=True)
            m_sc[i, hh] = m_new
            a_buf[slot, hh] = alpha
            p_buf[slot, hh] = pexp.astype(BF16)

    def values(n, slot):
        i, _, ks = tile_of(n)
        vt = vt_ref[0, 0, :, pl.ds(ks, tk)]
        for hh in range(2):
            rows = slice(hh * HEAD_DIM, (hh + 1) * HEAD_DIM)
            acc_sc[i, rows, :] = a_buf[slot, hh] * acc_sc[i, rows, :] + jnp.dot(
                vt[rows, :], p_buf[slot, hh], preferred_element_type=F32)

    def step_pair(masked):
        def body(t, carry):
            n = 2 * t
            scores(n + 1, 1)
            softmax(n, 0, masked)
            values(jnp.maximum(n - 1, 0), 1)
            scores(n + 2, 0)
            softmax(n + 1, 1, masked)
            values(n, 0)
            return carry
        return body

    scores(0, 0)
    lax.fori_loop(0, n_diag // 2, step_pair(True), 0)
    lax.fori_loop(n_diag // 2, n_steps // 2, step_pair(False), 0)
    values(n_steps - 1, 1)

    def finalize(i, carry):
        out_t = jnp.concatenate([acc_sc[i, :HEAD_DIM, :] / l_sc[i, 0],
                                 acc_sc[i, HEAD_DIM:, :] / l_sc[i, 1]], axis=0)
        o_ref[0, pl.ds(pl.multiple_of(i * tq, tq), tq), :] = out_t.T.astype(BF16)
        return carry

    lax.fori_loop(0, n_diag, finalize, 0)


def _attention_steps(s, tq, tk):
    nq = s // tq
    diag = [(i, i * tq // tk) for i in range(nq)]
    full = [(i, j) for i in range(nq) for j in range(i * tq // tk)]
    steps = diag + full
    assert nq % 2 == 0 and len(steps) % 2 == 0, "the pipeline advances two steps per iteration"
    padded = steps + [steps[-1]] * 2
    ti = jnp.asarray([i for i, _ in padded], jnp.int32)
    tj = jnp.asarray([j for _, j in padded], jnp.int32)
    return ti, tj, nq, len(steps)


def _attention(q, k, v_t, cum_col):
    bsz, pairs, _, s, _ = q.shape
    tq = min(ATTN_Q_TILE, s)
    tk = min(ATTN_K_TILE, s)
    assert tk % tq == 0
    ti, tj, nq, n_steps = _attention_steps(s, tq, tk)
    kern = functools.partial(_attn_kernel, tq=tq, tk=tk, n_diag=nq, n_steps=n_steps)
    grid_spec = pltpu.PrefetchScalarGridSpec(
        num_scalar_prefetch=2,
        grid=(bsz, pairs),
        in_specs=[pl.BlockSpec((1, 1, 2, s, LANES), lambda b, p, ti, tj: (b, p, 0, 0, 0)),
                  pl.BlockSpec((1, 1, s, LANES), lambda b, p, ti, tj: (b, p, 0, 0)),
                  pl.BlockSpec((1, 1, LANES, s), lambda b, p, ti, tj: (b, p, 0, 0)),
                  pl.BlockSpec((1, 1, s, 2), lambda b, p, ti, tj: (b, p, 0, 0))],
        out_specs=pl.BlockSpec((1, s, LANES), lambda b, p, ti, tj: (b, 0, p)),
        scratch_shapes=[pltpu.VMEM((2, 2, tk, tq), F32),
                        pltpu.VMEM((2, 2, tk, tq), BF16),
                        pltpu.VMEM((2, 2, 1, tq), F32),
                        pltpu.VMEM((nq, 2, 1, tq), F32),
                        pltpu.VMEM((nq, 2, 1, tq), F32),
                        pltpu.VMEM((nq, LANES, tq), F32)])
    return pl.pallas_call(
        kern,
        grid_spec=grid_spec,
        out_shape=jax.ShapeDtypeStruct((bsz, s, pairs * LANES), BF16),
        compiler_params=_params("arbitrary", "arbitrary"),
        name="forgetting_attention",
    )(ti, tj, q, k, v_t, cum_col)


def _merge_kernel(x_ref, mod_ref, gpre_ref, gpost_ref, wgate_ref, ys_ref, ya_ref,
                  wpa_ref, wpb_ref, wo_ref, o_ref):
    b = pl.program_id(0)
    x = x_ref[0]
    d = x.shape[-1]
    h = _rmsnorm(x, gpre_ref[...]) * (1.0 + _mod_row(mod_ref, 1, b)) + _mod_row(mod_ref, 0, b)
    gates = jnp.dot(h.astype(BF16), wgate_ref[...], preferred_element_type=F32)
    pa = jnp.dot(ys_ref[0], wpa_ref[...], preferred_element_type=F32)
    pb = jnp.dot(ya_ref[0], wpb_ref[...], preferred_element_type=F32)
    merged = jax.nn.sigmoid(gates[:, :d]) * pa + jax.nn.sigmoid(gates[:, d:]) * pb
    y = jnp.dot(merged.astype(BF16), wo_ref[...], preferred_element_type=F32)
    o_ref[0] = x + _mod_row(mod_ref, 2, b) * _rmsnorm(y, gpost_ref[...])


def _merge(x, mod, g_pre, g_post, w_gate, y_ssm, y_att, w_pa, w_pb, w_o):
    bsz, s, d = x.shape
    tm = min(TOKEN_TILE, s)
    width = y_att.shape[-1]
    full = lambda a: pl.BlockSpec(a.shape, lambda b, i: (0,) * a.ndim)
    g_pre = g_pre.reshape(1, d)
    g_post = g_post.reshape(1, d)
    return pl.pallas_call(
        _merge_kernel,
        grid=(bsz, s // tm),
        in_specs=[pl.BlockSpec((1, tm, d), lambda b, i: (b, i, 0)),
                  full(mod), full(g_pre), full(g_post), full(w_gate),
                  pl.BlockSpec((1, tm, width), lambda b, i: (b, i, 0)),
                  pl.BlockSpec((1, tm, width), lambda b, i: (b, i, 0)),
                  full(w_pa), full(w_pb), full(w_o)],
        out_specs=pl.BlockSpec((1, tm, d), lambda b, i: (b, i, 0)),
        out_shape=jax.ShapeDtypeStruct(x.shape, F32),
        compiler_params=_params("arbitrary", "arbitrary"),
        name="gated_merge",
    )(x, mod, g_pre, g_post, w_gate, y_ssm, y_att, w_pa, w_pb, w_o)


def _ffn_chunks(d_ff):
    mxu = 2 * LANES
    n_tiles = d_ff // mxu
    first = (n_tiles + 1) // 2 * mxu
    return ((0, first), (first, d_ff)) if first < d_ff else ((0, d_ff),)


def _ffn_kernel(x_ref, mod_ref, gpre_ref, gpost_ref, wg_ref, wu_ref, wd_ref, o_ref, *, chunks, sub):
    b = pl.program_id(0)
    for r0 in range(0, x_ref.shape[1], sub):
        x = x_ref[0, r0:r0 + sub, :]
        h = _rmsnorm(x, gpre_ref[...]) * (1.0 + _mod_row(mod_ref, 4, b)) + _mod_row(mod_ref, 3, b)
        hb = h.astype(BF16)
        acc = None
        for c0, c1 in chunks:
            g = jnp.dot(hb, wg_ref[:, c0:c1], preferred_element_type=F32)
            up = jnp.dot(hb, wu_ref[:, c0:c1], preferred_element_type=F32)
            a = (g * jax.nn.sigmoid(g) * up).astype(BF16)
            part = jnp.dot(a, wd_ref[c0:c1, :], preferred_element_type=F32)
            acc = part if acc is None else acc + part
        o_ref[0, r0:r0 + sub, :] = x + _mod_row(mod_ref, 5, b) * _rmsnorm(acc, gpost_ref[...])


def _resident(a):
    return pl.BlockSpec(a.shape, lambda b, i: (0,) * a.ndim, pipeline_mode=pl.Buffered(1))


def _ffn(x, mod, g_pre, g_post, w_gate, w_up, w_down):
    bsz, s, d = x.shape
    tm = min(2 * TOKEN_TILE, s)
    kern = functools.partial(_ffn_kernel, chunks=_ffn_chunks(w_gate.shape[1]),
                             sub=min(TOKEN_TILE, tm))
    full = lambda a: pl.BlockSpec(a.shape, lambda b, i: (0,) * a.ndim)
    g_pre = g_pre.reshape(1, d)
    g_post = g_post.reshape(1, d)
    return pl.pallas_call(
        kern,
        grid=(bsz, s // tm),
        in_specs=[pl.BlockSpec((1, tm, d), lambda b, i: (b, i, 0)),
                  full(mod), full(g_pre), full(g_post),
                  _resident(w_gate), _resident(w_up), _resident(w_down)],
        out_specs=pl.BlockSpec((1, tm, d), lambda b, i: (b, i, 0)),
        out_shape=jax.ShapeDtypeStruct(x.shape, F32),
        compiler_params=_params("arbitrary", "arbitrary"),
        name="swiglu_ffn",
    )(x, mod, g_pre, g_post, w_gate, w_up, w_down)


def kernel(x, c, w_ada, b_ada, g_pre_mix, g_post_mix, g_pre_ffn, g_post_ffn, w_in, lam_re, lam_im, log_dt, b_re, b_im, c_re, c_im, d_skip, w_glu, b_glu, b_f, w_pa, w_pb, w_o, w_ffn_gate, w_ffn_up, w_ffn_down):
    bsz, s, d = x.shape
    depth = w_in.shape[0]
    ssm_w = d_skip.shape[1]
    heads = b_f.shape[1]
    attn_w = heads * HEAD_DIM
    pairs = attn_w // LANES
    main_w = ssm_w + 3 * attn_w
    assert bsz == SUBLANES, "the S5 scan keeps the batch on the sublane axis"

    mod_all = _modulation(c, w_ada, b_ada)
    for l in range(depth):
        mod = mod_all[l]
        uqk_w = ssm_w + 2 * attn_w
        w_main = w_in[l, :, :uqk_w].astype(BF16)
        w_vf_t = w_in[l, :, uqk_w:main_w + heads].T.astype(BF16)
        w_gate = w_in[l, :, main_w + heads:].astype(BF16)

        u, q, k, v_t, f_t = _in_projection(x, mod, g_pre_mix[l], w_main, w_vf_t, ssm_w, attn_w)

        cum_col = _forget_cumsum(f_t, b_f[l])
        y_att = _attention(q, k, v_t, cum_col)

        bmat, a_re, a_im, cmat = _s5_matrices(lam_re[l], lam_im[l], log_dt[l], b_re[l], b_im[l],
                                              c_re[l], c_im[l], bsz)
        y_ssm = _ssm_branch(u, bmat, a_re, a_im, cmat,
                            d_skip[l], w_glu[l].astype(BF16), b_glu[l])

        x = _merge(x, mod, g_pre_mix[l], g_post_mix[l], w_gate, y_ssm, y_att,
                   w_pa[l].astype(BF16), w_pb[l].astype(BF16), w_o[l].astype(BF16))
        x = _ffn(x, mod, g_pre_ffn[l], g_post_ffn[l], w_ffn_gate[l].astype(BF16),
                 w_ffn_up[l].astype(BF16), w_ffn_down[l].astype(BF16))
    return x
```

```python
import functools
import math

import jax
import jax.numpy as jnp
from jax import lax
from jax.experimental import pallas as pl
from jax.experimental.pallas import tpu as pltpu

F32 = jnp.float32
BF16 = jnp.bfloat16

SSM_GROUP_CH = 16
SSM_STATE = 64
HEAD_DIM = 64
N_MOD = 6
RMS_EPS = 1e-6
EIG_CLIP = 1e-4
LOG2E = math.log2(math.e)

LANES = 128
SUBLANES = 8
SLAB = 2 * LANES
VMEM_LIMIT = 56 * 1024 * 1024

TOKEN_TILE = 512
SCAN_STEPS = 64
ATTN_Q_TILE = 512
ATTN_K_TILE = 512
SCAN_GROUP = 4
SCAN_UNROLL = 4


def _params(*sem):
    return pltpu.CompilerParams(dimension_semantics=sem, vmem_limit_bytes=VMEM_LIMIT)


def _rmsnorm(x, g):
    r = lax.rsqrt(jnp.mean(x * x, axis=-1, keepdims=True) + RMS_EPS)
    return x * r * g


def _mod_row(mod_ref, j, b):
    return mod_ref[j, pl.ds(b, 1), :]


def _mod_kernel(c_ref, w_ref, b_ref, o_ref):
    c = c_ref[...]
    cond = c * jax.nn.sigmoid(c)
    o_ref[0, 0] = jnp.dot(cond.astype(BF16), w_ref[0].astype(BF16),
                          preferred_element_type=F32) + b_ref[0, 0]


def _modulation(c, w_ada, b_ada):
    depth, d, nd = w_ada.shape
    bsz = c.shape[0]
    n = nd // d
    return pl.pallas_call(
        _mod_kernel,
        grid=(depth, n),
        in_specs=[pl.BlockSpec((bsz, d), lambda l, j: (0, 0)),
                  pl.BlockSpec((1, d, d), lambda l, j: (l, 0, j)),
                  pl.BlockSpec((1, 1, 1, d), lambda l, j: (l, j, 0, 0))],
        out_specs=pl.BlockSpec((1, 1, bsz, d), lambda l, j: (l, j, 0, 0)),
        out_shape=jax.ShapeDtypeStruct((depth, n, bsz, d), F32),
        compiler_params=_params("arbitrary", "arbitrary"),
        name="adaln_mod",
    )(c, w_ada, b_ada.reshape(depth, n, 1, d))


def _in_kernel(x_ref, mod_ref, g_ref, w_ref, wvf_ref, u_ref, q_ref, k_ref, vt_ref, f_ref,
               *, ssm_w, attn_w):
    b = pl.program_id(0)
    x = x_ref[0]
    h = _rmsnorm(x, g_ref[...]) * (1.0 + _mod_row(mod_ref, 1, b)) + _mod_row(mod_ref, 0, b)
    hb = h.astype(BF16)
    proj = jnp.dot(hb, w_ref[...], preferred_element_type=F32)
    u_ref[0] = proj[:, :ssm_w]
    scale = HEAD_DIM ** -0.5 * LOG2E
    proj_t = lax.dot_general(wvf_ref[...], hb, (((1,), (1,)), ((), ())),
                             preferred_element_type=F32)
    lane = lax.broadcasted_iota(jnp.int32, (x.shape[0], LANES), 1)
    for p in range(attn_w // LANES):
        c0 = ssm_w + p * LANES
        q_pair = proj[:, c0:c0 + LANES] * scale
        q_ref[0, p, 0] = jnp.where(lane < HEAD_DIM, q_pair, 0.0).astype(BF16)
        q_ref[0, p, 1] = jnp.where(lane >= HEAD_DIM, q_pair, 0.0).astype(BF16)
        k_ref[0, p] = proj[:, c0 + attn_w:c0 + attn_w + LANES].astype(BF16)
        vt_ref[0, p] = proj_t[p * LANES:(p + 1) * LANES, :].astype(BF16)
    f_ref[0] = proj_t[attn_w:, :]


def _in_projection(x, mod, g, w_main, w_vf_t, ssm_w, attn_w):
    bsz, s, d = x.shape
    tm = min(TOKEN_TILE, s)
    pairs = attn_w // LANES
    heads = w_vf_t.shape[0] - attn_w
    kern = functools.partial(_in_kernel, ssm_w=ssm_w, attn_w=attn_w)
    qk_shape = jax.ShapeDtypeStruct((bsz, pairs, s, LANES), BF16)
    qk_spec = pl.BlockSpec((1, pairs, tm, LANES), lambda b, i: (b, 0, i, 0))
    return pl.pallas_call(
        kern,
        grid=(bsz, s // tm),
        in_specs=[pl.BlockSpec((1, tm, d), lambda b, i: (b, i, 0)),
                  pl.BlockSpec(mod.shape, lambda b, i: (0, 0, 0)),
                  pl.BlockSpec((1, d), lambda b, i: (0, 0)),
                  pl.BlockSpec(w_main.shape, lambda b, i: (0, 0)),
                  pl.BlockSpec(w_vf_t.shape, lambda b, i: (0, 0))],
        out_specs=[pl.BlockSpec((1, tm, ssm_w), lambda b, i: (b, i, 0)),
                   pl.BlockSpec((1, pairs, 2, tm, LANES), lambda b, i: (b, 0, 0, i, 0)),
                   qk_spec,
                   pl.BlockSpec((1, pairs, LANES, tm), lambda b, i: (b, 0, 0, i)),
                   pl.BlockSpec((1, heads, tm), lambda b, i: (b, 0, i))],
        out_shape=[jax.ShapeDtypeStruct((bsz, s, ssm_w), F32),
                   jax.ShapeDtypeStruct((bsz, pairs, 2, s, LANES), BF16),
                   qk_shape,
                   jax.ShapeDtypeStruct((bsz, pairs, LANES, s), BF16),
                   jax.ShapeDtypeStruct((bsz, heads, s), F32)],
        compiler_params=_params("arbitrary", "arbitrary"),
        name="in_projection",
    )(x, mod, g.reshape(1, d), w_main, w_vf_t)


def _cum_kernel(f_ref, bf_ref, o_ref):
    x = jax.nn.log_sigmoid(f_ref[0] + bf_ref[...])
    s = x.shape[1]
    lane = lax.broadcasted_iota(jnp.int32, x.shape, 1)
    shift = 1
    while shift < s:
        x = x + jnp.where(lane >= shift, pltpu.roll(x, shift, axis=1), 0.0)
        shift *= 2
    cum_t = (x * LOG2E).T
    for p in range(o_ref.shape[1]):
        o_ref[0, p] = cum_t[:, 2 * p:2 * p + 2]


def _forget_cumsum(f_t, b_f):
    bsz, heads, s = f_t.shape
    pairs = heads // 2
    return pl.pallas_call(
        _cum_kernel,
        grid=(bsz,),
        in_specs=[pl.BlockSpec((1, heads, s), lambda b: (b, 0, 0)),
                  pl.BlockSpec((heads, 1), lambda b: (0, 0))],
        out_specs=pl.BlockSpec((1, pairs, s, 2), lambda b: (b, 0, 0, 0)),
        out_shape=jax.ShapeDtypeStruct((bsz, pairs, s, 2), F32),
        compiler_params=_params("arbitrary"),
        name="forget_cumsum",
    )(f_t, b_f.reshape(heads, 1))


def _ssm_kernel(u_ref, bm_ref, are_ref, aim_ref, cm_ref, dsk_ref, wg_ref, bg_ref, o_ref,
                buf, st, *, steps, n_slabs):
    @pl.when(pl.program_id(0) == 0)
    def _():
        st[...] = jnp.zeros_like(st)

    bsz, _, width = u_ref.shape
    rows = steps * bsz
    u = jnp.swapaxes(u_ref[...], 0, 1).reshape(rows, width)
    ub = u.astype(BF16)
    slabs_per_block = LANES // (2 * SSM_GROUP_CH)
    for j in range(n_slabs):
        m = j // slabs_per_block
        buf[j] = jnp.dot(ub[:, m * LANES:(m + 1) * LANES], bm_ref[j],
                         preferred_element_type=F32)

    for first in range(0, n_slabs, SCAN_GROUP):
        slabs = list(range(first, first + SCAN_GROUP))
        coefs = [(are_ref[j], aim_ref[j]) for j in slabs]
        init = tuple((st[j, :, :LANES], st[j, :, LANES:]) for j in slabs)

        def body(t, carry, slabs=slabs, coefs=coefs):
            row = t * SUBLANES
            new = []
            for (re, im), (are, aim), j in zip(carry, coefs, slabs):
                bre = buf[j, pl.ds(row, SUBLANES), :LANES]
                bim = buf[j, pl.ds(row, SUBLANES), LANES:]
                nre = are * re - aim * im + bre
                nim = are * im + aim * re + bim
                buf[j, pl.ds(row, SUBLANES), :LANES] = nre
                buf[j, pl.ds(row, SUBLANES), LANES:] = nim
                new.append((nre, nim))
            return tuple(new)

        fin = init
        for t in range(steps):
            fin = body(t, fin)
        for (re, im), j in zip(fin, slabs):
            st[j, :, :LANES] = re
            st[j, :, LANES:] = im

    ys = []
    for m in range(n_slabs // slabs_per_block):
        acc = None
        for j in range(m * slabs_per_block, (m + 1) * slabs_per_block):
            part = jnp.dot(buf[j].astype(BF16), cm_ref[j], preferred_element_type=F32)
            acc = part if acc is None else acc + part
        ys.append(acc)
    y = jnp.concatenate(ys, axis=1) + dsk_ref[...] * u
    z = jax.nn.gelu(y, approximate=True)
    gate = jax.nn.sigmoid(
        jnp.dot(z.astype(BF16), wg_ref[...], preferred_element_type=F32) + bg_ref[...])
    out = (z * gate).reshape(steps, bsz, width)
    o_ref[...] = jnp.swapaxes(out, 0, 1).astype(BF16)


def _ssm_branch(u, bmat, a_re, a_im, cmat, d_skip, w_glu, b_glu):
    bsz, s, width = u.shape
    steps = min(SCAN_STEPS, s)
    rows = steps * bsz
    n_slabs = bmat.shape[0]
    kern = functools.partial(_ssm_kernel, steps=steps, n_slabs=n_slabs)
    full = lambda a: pl.BlockSpec(a.shape, lambda i: (0,) * a.ndim)
    d_skip = d_skip.reshape(1, width)
    b_glu = b_glu.reshape(1, width)
    return pl.pallas_call(
        kern,
        grid=(s // steps,),
        in_specs=[pl.BlockSpec((bsz, steps, width), lambda i: (0, i, 0)),
                  full(bmat), full(a_re), full(a_im), full(cmat),
                  full(d_skip), full(w_glu), full(b_glu)],
        out_specs=pl.BlockSpec((bsz, steps, width), lambda i: (0, i, 0)),
        out_shape=jax.ShapeDtypeStruct((bsz, s, width), BF16),
        scratch_shapes=[pltpu.VMEM((n_slabs, rows, SLAB), F32),
                        pltpu.VMEM((n_slabs, bsz, SLAB), F32)],
        compiler_params=_params("arbitrary"),
        name="s5_branch",
    )(u, bmat, a_re, a_im, cmat, d_skip, w_glu, b_glu)


def _s5_matrices(lam_re, lam_im, log_dt, b_re, b_im, c_re, c_im, bsz):
    g, p = lam_re.shape
    h = b_re.shape[-1]
    lr = jnp.minimum(lam_re, -EIG_CLIP)
    li = lam_im
    dt = jnp.exp(log_dt)[:, None]
    mag = jnp.exp(lr * dt)
    a_re = mag * jnp.cos(li * dt)
    a_im = mag * jnp.sin(li * dt)
    den = lr * lr + li * li
    f_re = ((a_re - 1.0) * lr + a_im * li) / den
    f_im = (a_im * lr - (a_re - 1.0) * li) / den
    bb_re = f_re[..., None] * b_re - f_im[..., None] * b_im
    bb_im = f_re[..., None] * b_im + f_im[..., None] * b_re
    n_slabs = g // 2
    per_block = LANES // (2 * h)
    n_blocks = n_slabs // per_block
    eye_s = jnp.eye(per_block, dtype=F32)
    eye_g = jnp.eye(2, dtype=F32)

    def expand(w_hp, spec):
        w = w_hp.reshape(n_blocks, per_block, 2, h, p)
        return jnp.einsum(spec, w, eye_s, eye_g)

    b_spec = "mjghp,ja,gk->mjaghkp"
    bmat = jnp.concatenate(
        [expand(bb.transpose(0, 2, 1), b_spec).reshape(n_slabs, LANES, LANES)
         for bb in (bb_re, bb_im)], axis=2)
    c_spec = "mjghp,ja,gk->mjkpagh"
    cmat = jnp.concatenate(
        [expand(cc, c_spec).reshape(n_slabs, LANES, LANES) for cc in (c_re, -c_im)], axis=1)
    a_re_s = a_re.reshape(n_slabs, 1, 2 * p)
    a_im_s = a_im.reshape(n_slabs, 1, 2 * p)
    a_re_s = jnp.broadcast_to(a_re_s, (n_slabs, bsz, 2 * p))
    a_im_s = jnp.broadcast_to(a_im_s, (n_slabs, bsz, 2 * p))
    return bmat.astype(BF16), a_re_s, a_im_s, cmat.astype(BF16)


def _attn_kernel(ti_ref, tj_ref, q_ref, k_ref, vt_ref, ccol_ref, o_ref,
                 s_buf, p_buf, a_buf, m_sc, l_sc, acc_sc, *, tq, tk, n_diag, n_steps):
    m_sc[...] = jnp.full_like(m_sc, -jnp.inf)
    l_sc[...] = jnp.zeros_like(l_sc)
    acc_sc[...] = jnp.zeros_like(acc_sc)
    p_buf[1] = jnp.zeros(p_buf.shape[1:], BF16)
    a_buf[1] = jnp.ones(a_buf.shape[1:], F32)

    def tile_of(n):
        i = ti_ref[n]
        return i, pl.multiple_of(i * tq, tq), pl.multiple_of(tj_ref[n] * tk, tk)

    def scores(n, slot, masked):
        _, qs, ks = tile_of(n)
        kt = k_ref[0, 0, pl.ds(ks, tk), :]
        for hh in range(2):
            ck = ccol_ref[0, 0, pl.ds(ks, tk), hh:hh + 1]
            s = lax.dot_general(kt, q_ref[0, 0, hh, pl.ds(qs, tq), :], (((1,), (1,)), ((), ())),
                                preferred_element_type=F32) - ck
            if masked:
                kpos = ks + lax.broadcasted_iota(jnp.int32, s.shape, 0)
                qpos = qs + lax.broadcasted_iota(jnp.int32, s.shape, 1)
                s = jnp.where(kpos <= qpos, s, -jnp.inf)
            s_buf[slot, hh] = s

    def softmax(n, slot):
        i, _, _ = tile_of(n)
        for hh in range(2):
            m_old = m_sc[i, hh]
            m_new = jnp.maximum(m_old, jnp.max(s_buf[slot, hh], axis=0, keepdims=True))
            m_sc[i, hh] = m_new
            alpha = jnp.exp2(m_old - m_new)
            pexp = jnp.exp2(s_buf[slot, hh] - m_new)
            l_sc[i, hh] = alpha * l_sc[i, hh] + jnp.sum(pexp, axis=0, keepdims=True)
            a_buf[slot, hh] = alpha
            p_buf[slot, hh] = pexp.astype(BF16)

    def values(n, slot):
        i, _, ks = tile_of(n)
        vt = vt_ref[0, 0, :, pl.ds(ks, tk)]
        for hh in range(2):
            rows = slice(hh * HEAD_DIM, (hh + 1) * HEAD_DIM)
            acc_sc[i, rows, :] = a_buf[slot, hh] * acc_sc[i, rows, :] + jnp.dot(
                vt[rows, :], p_buf[slot, hh], preferred_element_type=F32)

    def step_pair(mask_odd, mask_even):
        def body(t, carry):
            n = 2 * t
            scores(n + 1, 1, mask_odd)
            softmax(n, 0)
            values(jnp.maximum(n - 1, 0), 1)
            scores(n + 2, 0, mask_even)
            softmax(n + 1, 1)
            values(n, 0)
            return carry
        return body

    last_diag_pair = n_diag // 2 - 1
    scores(0, 0, True)
    lax.fori_loop(0, last_diag_pair, step_pair(True, True), 0)
    step_pair(True, False)(last_diag_pair, 0)
    lax.fori_loop(n_diag // 2, n_steps // 2, step_pair(False, False), 0)
    values(n_steps - 1, 1)

    def finalize(i, carry):
        out_t = jnp.concatenate([acc_sc[i, :HEAD_DIM, :] / l_sc[i, 0],
                                 acc_sc[i, HEAD_DIM:, :] / l_sc[i, 1]], axis=0)
        o_ref[0, pl.ds(pl.multiple_of(i * tq, tq), tq), :] = out_t.T.astype(BF16)
        return carry

    lax.fori_loop(0, n_diag, finalize, 0)


def _attention_steps(s, tq, tk):
    nq = s // tq
    diag = [(i, i * tq // tk) for i in range(nq)]
    full = [(i, j) for i in range(nq) for j in range(i * tq // tk)]
    steps = diag + full
    assert nq % 2 == 0 and len(steps) % 2 == 0, "the pipeline advances two steps per iteration"
    padded = steps + [steps[-1]] * 2
    ti = jnp.asarray([i for i, _ in padded], jnp.int32)
    tj = jnp.asarray([j for _, j in padded], jnp.int32)
    return ti, tj, nq, len(steps)


def _attention(q, k, v_t, cum_col):
    bsz, pairs, _, s, _ = q.shape
    tq = min(ATTN_Q_TILE, s)
    tk = min(ATTN_K_TILE, s)
    assert tk % tq == 0
    ti, tj, nq, n_steps = _attention_steps(s, tq, tk)
    kern = functools.partial(_attn_kernel, tq=tq, tk=tk, n_diag=nq, n_steps=n_steps)
    grid_spec = pltpu.PrefetchScalarGridSpec(
        num_scalar_prefetch=2,
        grid=(bsz, pairs),
        in_specs=[pl.BlockSpec((1, 1, 2, s, LANES), lambda b, p, ti, tj: (b, p, 0, 0, 0)),
                  pl.BlockSpec((1, 1, s, LANES), lambda b, p, ti, tj: (b, p, 0, 0)),
                  pl.BlockSpec((1, 1, LANES, s), lambda b, p, ti, tj: (b, p, 0, 0)),
                  pl.BlockSpec((1, 1, s, 2), lambda b, p, ti, tj: (b, p, 0, 0))],
        out_specs=pl.BlockSpec((1, s, LANES), lambda b, p, ti, tj: (b, 0, p)),
        scratch_shapes=[pltpu.VMEM((2, 2, tk, tq), F32),
                        pltpu.VMEM((2, 2, tk, tq), BF16),
                        pltpu.VMEM((2, 2, 1, tq), F32),
                        pltpu.VMEM((nq, 2, 1, tq), F32),
                        pltpu.VMEM((nq, 2, 1, tq), F32),
                        pltpu.VMEM((nq, LANES, tq), F32)])
    return pl.pallas_call(
        kern,
        grid_spec=grid_spec,
        out_shape=jax.ShapeDtypeStruct((bsz, s, pairs * LANES), BF16),
        compiler_params=_params("arbitrary", "arbitrary"),
        name="forgetting_attention",
    )(ti, tj, q, k, v_t, cum_col)


def _merge_kernel(x_ref, mod_ref, gpre_ref, gpost_ref, wgate_ref, ys_ref, ya_ref,
                  wpa_ref, wpb_ref, wo_ref, o_ref, *, sub):
    b = pl.program_id(0)
    d = x_ref.shape[-1]
    for r0 in range(0, x_ref.shape[1], sub):
        rows = slice(r0, r0 + sub)
        x = x_ref[0, rows, :]
        h = _rmsnorm(x, gpre_ref[...]) * (1.0 + _mod_row(mod_ref, 1, b)) + _mod_row(mod_ref, 0, b)
        gates = jnp.dot(h.astype(BF16), wgate_ref[...], preferred_element_type=F32)
        pa = jnp.dot(ys_ref[0, rows, :], wpa_ref[...], preferred_element_type=F32)
        pb = jnp.dot(ya_ref[0, rows, :], wpb_ref[...], preferred_element_type=F32)
        merged = jax.nn.sigmoid(gates[:, :d]) * pa + jax.nn.sigmoid(gates[:, d:]) * pb
        y = jnp.dot(merged.astype(BF16), wo_ref[...], preferred_element_type=F32)
        o_ref[0, rows, :] = x + _mod_row(mod_ref, 2, b) * _rmsnorm(y, gpost_ref[...])


def _resident(a):
    return pl.BlockSpec(a.shape, lambda b, i: (0,) * a.ndim, pipeline_mode=pl.Buffered(1))


def _merge(x, mod, g_pre, g_post, w_gate, y_ssm, y_att, w_pa, w_pb, w_o):
    bsz, s, d = x.shape
    tm = min(2 * TOKEN_TILE, s)
    width = y_att.shape[-1]
    full = lambda a: pl.BlockSpec(a.shape, lambda b, i: (0,) * a.ndim)
    g_pre = g_pre.reshape(1, d)
    g_post = g_post.reshape(1, d)
    return pl.pallas_call(
        functools.partial(_merge_kernel, sub=min(TOKEN_TILE, tm)),
        grid=(bsz, s // tm),
        in_specs=[pl.BlockSpec((1, tm, d), lambda b, i: (b, i, 0)),
                  full(mod), full(g_pre), full(g_post), _resident(w_gate),
                  pl.BlockSpec((1, tm, width), lambda b, i: (b, i, 0)),
                  pl.BlockSpec((1, tm, width), lambda b, i: (b, i, 0)),
                  _resident(w_pa), _resident(w_pb), _resident(w_o)],
        out_specs=pl.BlockSpec((1, tm, d), lambda b, i: (b, i, 0)),
        out_shape=jax.ShapeDtypeStruct(x.shape, F32),
        compiler_params=_params("arbitrary", "arbitrary"),
        name="gated_merge",
    )(x, mod, g_pre, g_post, w_gate, y_ssm, y_att, w_pa, w_pb, w_o)


def _ffn_chunks(d_ff):
    mxu = 2 * LANES
    n_tiles = d_ff // mxu
    first = (n_tiles + 1) // 2 * mxu
    return ((0, first), (first, d_ff)) if first < d_ff else ((0, d_ff),)


def _ffn_kernel(x_ref, mod_ref, gpre_ref, gpost_ref, wg_ref, wu_ref, wd_ref, o_ref, *, chunks, sub):
    b = pl.program_id(0)
    for r0 in range(0, x_ref.shape[1], sub):
        x = x_ref[0, r0:r0 + sub, :]
        h = _rmsnorm(x, gpre_ref[...]) * (1.0 + _mod_row(mod_ref, 4, b)) + _mod_row(mod_ref, 3, b)
        hb = h.astype(BF16)
        acc = None
        for c0, c1 in chunks:
            g = jnp.dot(hb, wg_ref[:, c0:c1], preferred_element_type=F32)
            up = jnp.dot(hb, wu_ref[:, c0:c1], preferred_element_type=F32)
            a = (g * jax.nn.sigmoid(g) * up).astype(BF16)
            part = jnp.dot(a, wd_ref[c0:c1, :], preferred_element_type=F32)
            acc = part if acc is None else acc + part
        o_ref[0, r0:r0 + sub, :] = x + _mod_row(mod_ref, 5, b) * _rmsnorm(acc, gpost_ref[...])


def _ffn(x, mod, g_pre, g_post, w_gate, w_up, w_down):
    bsz, s, d = x.shape
    tm = min(2 * TOKEN_TILE, s)
    kern = functools.partial(_ffn_kernel, chunks=_ffn_chunks(w_gate.shape[1]),
                             sub=min(TOKEN_TILE, tm))
    full = lambda a: pl.BlockSpec(a.shape, lambda b, i: (0,) * a.ndim)
    g_pre = g_pre.reshape(1, d)
    g_post = g_post.reshape(1, d)
    return pl.pallas_call(
        kern,
        grid=(bsz, s // tm),
        in_specs=[pl.BlockSpec((1, tm, d), lambda b, i: (b, i, 0)),
                  full(mod), full(g_pre), full(g_post),
                  _resident(w_gate), _resident(w_up), _resident(w_down)],
        out_specs=pl.BlockSpec((1, tm, d), lambda b, i: (b, i, 0)),
        out_shape=jax.ShapeDtypeStruct(x.shape, F32),
        compiler_params=_params("arbitrary", "arbitrary"),
        name="swiglu_ffn",
    )(x, mod, g_pre, g_post, w_gate, w_up, w_down)


def kernel(x, c, w_ada, b_ada, g_pre_mix, g_post_mix, g_pre_ffn, g_post_ffn, w_in, lam_re, lam_im, log_dt, b_re, b_im, c_re, c_im, d_skip, w_glu, b_glu, b_f, w_pa, w_pb, w_o, w_ffn_gate, w_ffn_up, w_ffn_down):
    bsz, s, d = x.shape
    depth = w_in.shape[0]
    ssm_w = d_skip.shape[1]
    heads = b_f.shape[1]
    attn_w = heads * HEAD_DIM
    pairs = attn_w // LANES
    main_w = ssm_w + 3 * attn_w
    assert bsz == SUBLANES, "the S5 scan keeps the batch on the sublane axis"

    mod_all = _modulation(c, w_ada, b_ada)
    for l in range(depth):
        mod = mod_all[l]
        uqk_w = ssm_w + 2 * attn_w
        w_main = w_in[l, :, :uqk_w].astype(BF16)
        w_vf_t = w_in[l, :, uqk_w:main_w + heads].T.astype(BF16)
        w_gate = w_in[l, :, main_w + heads:].astype(BF16)

        u, q, k, v_t, f_t = _in_projection(x, mod, g_pre_mix[l], w_main, w_vf_t, ssm_w, attn_w)

        cum_col = _forget_cumsum(f_t, b_f[l])
        y_att = _attention(q, k, v_t, cum_col)

        bmat, a_re, a_im, cmat = _s5_matrices(lam_re[l], lam_im[l], log_dt[l], b_re[l], b_im[l],
                                              c_re[l], c_im[l], bsz)
        y_ssm = _ssm_branch(u, bmat, a_re, a_im, cmat,
                            d_skip[l], w_glu[l].astype(BF16), b_glu[l])

        x = _merge(x, mod, g_pre_mix[l], g_post_mix[l], w_gate, y_ssm, y_att,
                   w_pa[l].astype(BF16), w_pb[l].astype(BF16), w_o[l].astype(BF16))
        x = _ffn(x, mod, g_pre_ffn[l], g_post_ffn[l], w_ffn_gate[l].astype(BF16),
                 w_ffn_up[l].astype(BF16), w_ffn_down[l].astype(BF16))
    return x
```

```python
import functools
import math

import jax
import jax.numpy as jnp
from jax import lax
from jax.experimental import pallas as pl
from jax.experimental.pallas import tpu as pltpu

F32 = jnp.float32
BF16 = jnp.bfloat16

SSM_GROUP_CH = 16
SSM_STATE = 64
HEAD_DIM = 64
N_MOD = 6
RMS_EPS = 1e-6
EIG_CLIP = 1e-4
LOG2E = math.log2(math.e)

LANES = 128
SUBLANES = 8
SLAB = 2 * LANES
VMEM_LIMIT = 56 * 1024 * 1024

TOKEN_TILE = 512
SCAN_STEPS = 64
ATTN_Q_TILE = 512
ATTN_K_TILE = 512
SCAN_GROUP = 4
SCAN_UNROLL = 4


def _params(*sem):
    return pltpu.CompilerParams(dimension_semantics=sem, vmem_limit_bytes=VMEM_LIMIT)


def _rmsnorm(x, g):
    r = lax.rsqrt(jnp.mean(x * x, axis=-1, keepdims=True) + RMS_EPS)
    return x * r * g


def _mod_row(mod_ref, j, b):
    return mod_ref[j, pl.ds(b, 1), :]


def _mod_kernel(c_ref, w_ref, b_ref, o_ref):
    c = c_ref[...]
    cond = c * jax.nn.sigmoid(c)
    o_ref[0, 0] = jnp.dot(cond.astype(BF16), w_ref[0].astype(BF16),
                          preferred_element_type=F32) + b_ref[0, 0]


def _modulation(c, w_ada, b_ada):
    depth, d, nd = w_ada.shape
    bsz = c.shape[0]
    n = nd // d
    return pl.pallas_call(
        _mod_kernel,
        grid=(depth, n),
        in_specs=[pl.BlockSpec((bsz, d), lambda l, j: (0, 0)),
                  pl.BlockSpec((1, d, d), lambda l, j: (l, 0, j)),
                  pl.BlockSpec((1, 1, 1, d), lambda l, j: (l, j, 0, 0))],
        out_specs=pl.BlockSpec((1, 1, bsz, d), lambda l, j: (l, j, 0, 0)),
        out_shape=jax.ShapeDtypeStruct((depth, n, bsz, d), F32),
        compiler_params=_params("arbitrary", "arbitrary"),
        name="adaln_mod",
    )(c, w_ada, b_ada.reshape(depth, n, 1, d))


def _in_kernel(x_ref, mod_ref, g_ref, w_ref, wvf_ref, u_ref, q_ref, k_ref, vt_ref, f_ref,
               *, ssm_w, attn_w):
    b = pl.program_id(0)
    x = x_ref[0]
    h = _rmsnorm(x, g_ref[...]) * (1.0 + _mod_row(mod_ref, 1, b)) + _mod_row(mod_ref, 0, b)
    hb = h.astype(BF16)
    proj = jnp.dot(hb, w_ref[...], preferred_element_type=F32)
    u_ref[0] = proj[:, :ssm_w]
    scale = HEAD_DIM ** -0.5 * LOG2E
    proj_t = lax.dot_general(wvf_ref[...], hb, (((1,), (1,)), ((), ())),
                             preferred_element_type=F32)
    lane = lax.broadcasted_iota(jnp.int32, (x.shape[0], LANES), 1)
    for p in range(attn_w // LANES):
        c0 = ssm_w + p * LANES
        q_pair = proj[:, c0:c0 + LANES] * scale
        q_ref[0, p, 0] = jnp.where(lane < HEAD_DIM, q_pair, 0.0).astype(BF16)
        q_ref[0, p, 1] = jnp.where(lane >= HEAD_DIM, q_pair, 0.0).astype(BF16)
        k_ref[0, p] = proj[:, c0 + attn_w:c0 + attn_w + LANES].astype(BF16)
        vt_ref[0, p] = proj_t[p * LANES:(p + 1) * LANES, :].astype(BF16)
    f_ref[0] = proj_t[attn_w:, :]


def _in_projection(x, mod, g, w_main, w_vf_t, ssm_w, attn_w):
    bsz, s, d = x.shape
    tm = min(TOKEN_TILE, s)
    pairs = attn_w // LANES
    heads = w_vf_t.shape[0] - attn_w
    kern = functools.partial(_in_kernel, ssm_w=ssm_w, attn_w=attn_w)
    qk_shape = jax.ShapeDtypeStruct((bsz, pairs, s, LANES), BF16)
    qk_spec = pl.BlockSpec((1, pairs, tm, LANES), lambda b, i: (b, 0, i, 0))
    return pl.pallas_call(
        kern,
        grid=(bsz, s // tm),
        in_specs=[pl.BlockSpec((1, tm, d), lambda b, i: (b, i, 0)),
                  pl.BlockSpec(mod.shape, lambda b, i: (0, 0, 0)),
                  pl.BlockSpec((1, d), lambda b, i: (0, 0)),
                  pl.BlockSpec(w_main.shape, lambda b, i: (0, 0)),
                  pl.BlockSpec(w_vf_t.shape, lambda b, i: (0, 0))],
        out_specs=[pl.BlockSpec((1, tm, ssm_w), lambda b, i: (b, i, 0)),
                   pl.BlockSpec((1, pairs, 2, tm, LANES), lambda b, i: (b, 0, 0, i, 0)),
                   qk_spec,
                   pl.BlockSpec((1, pairs, LANES, tm), lambda b, i: (b, 0, 0, i)),
                   pl.BlockSpec((1, heads, tm), lambda b, i: (b, 0, i))],
        out_shape=[jax.ShapeDtypeStruct((bsz, s, ssm_w), F32),
                   jax.ShapeDtypeStruct((bsz, pairs, 2, s, LANES), BF16),
                   qk_shape,
                   jax.ShapeDtypeStruct((bsz, pairs, LANES, s), BF16),
                   jax.ShapeDtypeStruct((bsz, heads, s), F32)],
        compiler_params=_params("arbitrary", "arbitrary"),
        name="in_projection",
    )(x, mod, g.reshape(1, d), w_main, w_vf_t)


def _cum_kernel(f_ref, bf_ref, o_ref):
    x = jax.nn.log_sigmoid(f_ref[0] + bf_ref[...])
    s = x.shape[1]
    lane = lax.broadcasted_iota(jnp.int32, x.shape, 1)
    shift = 1
    while shift < s:
        x = x + jnp.where(lane >= shift, pltpu.roll(x, shift, axis=1), 0.0)
        shift *= 2
    cum_t = (x * LOG2E).T
    for p in range(o_ref.shape[1]):
        o_ref[0, p] = cum_t[:, 2 * p:2 * p + 2]


def _forget_cumsum(f_t, b_f):
    bsz, heads, s = f_t.shape
    pairs = heads // 2
    return pl.pallas_call(
        _cum_kernel,
        grid=(bsz,),
        in_specs=[pl.BlockSpec((1, heads, s), lambda b: (b, 0, 0)),
                  pl.BlockSpec((heads, 1), lambda b: (0, 0))],
        out_specs=pl.BlockSpec((1, pairs, s, 2), lambda b: (b, 0, 0, 0)),
        out_shape=jax.ShapeDtypeStruct((bsz, pairs, s, 2), F32),
        compiler_params=_params("arbitrary"),
        name="forget_cumsum",
    )(f_t, b_f.reshape(heads, 1))


def _ssm_kernel(u_ref, bm_ref, are_ref, aim_ref, cm_ref, dsk_ref, wg_ref, bg_ref, o_ref,
                buf, st, *, steps, n_slabs):
    @pl.when(pl.program_id(0) == 0)
    def _():
        st[...] = jnp.zeros_like(st)

    bsz, _, width = u_ref.shape
    rows = steps * bsz
    u = jnp.swapaxes(u_ref[...], 0, 1).reshape(rows, width)
    ub = u.astype(BF16)
    slabs_per_block = LANES // (2 * SSM_GROUP_CH)
    for j in range(n_slabs):
        m = j // slabs_per_block
        buf[j] = jnp.dot(ub[:, m * LANES:(m + 1) * LANES], bm_ref[j],
                         preferred_element_type=F32)

    for first in range(0, n_slabs, SCAN_GROUP):
        slabs = list(range(first, first + SCAN_GROUP))
        coefs = [(are_ref[j], aim_ref[j]) for j in slabs]
        init = tuple((st[j, :, :LANES], st[j, :, LANES:]) for j in slabs)

        def body(t, carry, slabs=slabs, coefs=coefs):
            row = t * SUBLANES
            new = []
            for (re, im), (are, aim), j in zip(carry, coefs, slabs):
                bre = buf[j, pl.ds(row, SUBLANES), :LANES]
                bim = buf[j, pl.ds(row, SUBLANES), LANES:]
                nre = are * re - aim * im + bre
                nim = are * im + aim * re + bim
                buf[j, pl.ds(row, SUBLANES), :LANES] = nre
                buf[j, pl.ds(row, SUBLANES), LANES:] = nim
                new.append((nre, nim))
            return tuple(new)

        fin = init
        for t in range(steps):
            fin = body(t, fin)
        for (re, im), j in zip(fin, slabs):
            st[j, :, :LANES] = re
            st[j, :, LANES:] = im

    ys = []
    for m in range(n_slabs // slabs_per_block):
        acc = None
        for j in range(m * slabs_per_block, (m + 1) * slabs_per_block):
            part = jnp.dot(buf[j].astype(BF16), cm_ref[j], preferred_element_type=F32)
            acc = part if acc is None else acc + part
        ys.append(acc)
    y = jnp.concatenate(ys, axis=1) + dsk_ref[...] * u
    z = jax.nn.gelu(y, approximate=True)
    gate = jax.nn.sigmoid(
        jnp.dot(z.astype(BF16), wg_ref[...], preferred_element_type=F32) + bg_ref[...])
    out = (z * gate).reshape(steps, bsz, width)
    o_ref[...] = jnp.swapaxes(out, 0, 1).astype(BF16)


def _ssm_branch(u, bmat, a_re, a_im, cmat, d_skip, w_glu, b_glu):
    bsz, s, width = u.shape
    steps = min(SCAN_STEPS, s)
    rows = steps * bsz
    n_slabs = bmat.shape[0]
    kern = functools.partial(_ssm_kernel, steps=steps, n_slabs=n_slabs)
    full = lambda a: pl.BlockSpec(a.shape, lambda i: (0,) * a.ndim)
    d_skip = d_skip.reshape(1, width)
    b_glu = b_glu.reshape(1, width)
    return pl.pallas_call(
        kern,
        grid=(s // steps,),
        in_specs=[pl.BlockSpec((bsz, steps, width), lambda i: (0, i, 0)),
                  full(bmat), full(a_re), full(a_im), full(cmat),
                  full(d_skip), full(w_glu), full(b_glu)],
        out_specs=pl.BlockSpec((bsz, steps, width), lambda i: (0, i, 0)),
        out_shape=jax.ShapeDtypeStruct((bsz, s, width), BF16),
        scratch_shapes=[pltpu.VMEM((n_slabs, rows, SLAB), F32),
                        pltpu.VMEM((n_slabs, bsz, SLAB), F32)],
        compiler_params=_params("arbitrary"),
        name="s5_branch",
    )(u, bmat, a_re, a_im, cmat, d_skip, w_glu, b_glu)


def _s5_matrices(lam_re, lam_im, log_dt, b_re, b_im, c_re, c_im, bsz):
    g, p = lam_re.shape
    h = b_re.shape[-1]
    lr = jnp.minimum(lam_re, -EIG_CLIP)
    li = lam_im
    dt = jnp.exp(log_dt)[:, None]
    mag = jnp.exp(lr * dt)
    a_re = mag * jnp.cos(li * dt)
    a_im = mag * jnp.sin(li * dt)
    den = lr * lr + li * li
    f_re = ((a_re - 1.0) * lr + a_im * li) / den
    f_im = (a_im * lr - (a_re - 1.0) * li) / den
    bb_re = f_re[..., None] * b_re - f_im[..., None] * b_im
    bb_im = f_re[..., None] * b_im + f_im[..., None] * b_re
    n_slabs = g // 2
    per_block = LANES // (2 * h)
    n_blocks = n_slabs // per_block
    eye_s = jnp.eye(per_block, dtype=F32)
    eye_g = jnp.eye(2, dtype=F32)

    def expand(w_hp, spec):
        w = w_hp.reshape(n_blocks, per_block, 2, h, p)
        return jnp.einsum(spec, w, eye_s, eye_g)

    b_spec = "mjghp,ja,gk->mjaghkp"
    bmat = jnp.concatenate(
        [expand(bb.transpose(0, 2, 1), b_spec).reshape(n_slabs, LANES, LANES)
         for bb in (bb_re, bb_im)], axis=2)
    c_spec = "mjghp,ja,gk->mjkpagh"
    cmat = jnp.concatenate(
        [expand(cc, c_spec).reshape(n_slabs, LANES, LANES) for cc in (c_re, -c_im)], axis=1)
    a_re_s = a_re.reshape(n_slabs, 1, 2 * p)
    a_im_s = a_im.reshape(n_slabs, 1, 2 * p)
    a_re_s = jnp.broadcast_to(a_re_s, (n_slabs, bsz, 2 * p))
    a_im_s = jnp.broadcast_to(a_im_s, (n_slabs, bsz, 2 * p))
    return bmat.astype(BF16), a_re_s, a_im_s, cmat.astype(BF16)


def _attn_kernel(ti_ref, tj_ref, q_ref, k_ref, vt_ref, ccol_ref, o_ref,
                 s_buf, p_buf, a_buf, m_sc, l_sc, acc_sc, *, tq, tk, n_diag, n_steps):
    m_sc[...] = jnp.full_like(m_sc, -jnp.inf)
    l_sc[...] = jnp.zeros_like(l_sc)
    acc_sc[...] = jnp.zeros_like(acc_sc)
    p_buf[1] = jnp.zeros(p_buf.shape[1:], BF16)
    a_buf[1] = jnp.ones(a_buf.shape[1:], F32)

    def tile_of(n):
        i = ti_ref[n]
        return i, pl.multiple_of(i * tq, tq), pl.multiple_of(tj_ref[n] * tk, tk)

    def scores(n, slot, masked):
        _, qs, ks = tile_of(n)
        kt = k_ref[0, 0, pl.ds(ks, tk), :]
        for hh in range(2):
            ck = ccol_ref[0, 0, pl.ds(ks, tk), hh:hh + 1]
            s = lax.dot_general(kt, q_ref[0, 0, hh, pl.ds(qs, tq), :], (((1,), (1,)), ((), ())),
                                preferred_element_type=F32) - ck
            if masked:
                kpos = ks + lax.broadcasted_iota(jnp.int32, s.shape, 0)
                qpos = qs + lax.broadcasted_iota(jnp.int32, s.shape, 1)
                s = jnp.where(kpos <= qpos, s, -jnp.inf)
            s_buf[slot, hh] = s

    def softmax(n, slot):
        i, _, _ = tile_of(n)
        for hh in range(2):
            m_old = m_sc[i, hh]
            m_new = jnp.maximum(m_old, jnp.max(s_buf[slot, hh], axis=0, keepdims=True))
            m_sc[i, hh] = m_new
            alpha = jnp.exp2(m_old - m_new)
            pexp = jnp.exp2(s_buf[slot, hh] - m_new)
            l_sc[i, hh] = alpha * l_sc[i, hh] + jnp.sum(pexp, axis=0, keepdims=True)
            a_buf[slot, hh] = alpha
            p_buf[slot, hh] = pexp.astype(BF16)

    def values(n, slot):
        i, _, ks = tile_of(n)
        vt = vt_ref[0, 0, :, pl.ds(ks, tk)]
        for hh in range(2):
            rows = slice(hh * HEAD_DIM, (hh + 1) * HEAD_DIM)
            acc_sc[i, rows, :] = a_buf[slot, hh] * acc_sc[i, rows, :] + jnp.dot(
                vt[rows, :], p_buf[slot, hh], preferred_element_type=F32)

    def step_pair(mask_odd, mask_even):
        def body(t, carry):
            n = 2 * t
            scores(n + 1, 1, mask_odd)
            softmax(n, 0)
            values(jnp.maximum(n - 1, 0), 1)
            scores(n + 2, 0, mask_even)
            softmax(n + 1, 1)
            values(n, 0)
            return carry
        return body

    last_diag_pair = n_diag // 2 - 1
    scores(0, 0, True)
    lax.fori_loop(0, last_diag_pair, step_pair(True, True), 0)
    step_pair(True, False)(last_diag_pair, 0)
    lax.fori_loop(n_diag // 2, n_steps // 2, step_pair(False, False), 0)
    values(n_steps - 1, 1)

    def finalize(i, carry):
        out_t = jnp.concatenate([acc_sc[i, :HEAD_DIM, :] / l_sc[i, 0],
                                 acc_sc[i, HEAD_DIM:, :] / l_sc[i, 1]], axis=0)
        o_ref[0, pl.ds(pl.multiple_of(i * tq, tq), tq), :] = out_t.T.astype(BF16)
        return carry

    lax.fori_loop(0, n_diag, finalize, 0)


def _attention_steps(s, tq, tk):
    nq = s // tq
    diag = [(i, i * tq // tk) for i in range(nq)]
    full = [(i, j) for i in range(nq) for j in range(i * tq // tk)]
    steps = diag + full
    assert nq % 2 == 0 and len(steps) % 2 == 0, "the pipeline advances two steps per iteration"
    padded = steps + [steps[-1]] * 2
    ti = jnp.asarray([i for i, _ in padded], jnp.int32)
    tj = jnp.asarray([j for _, j in padded], jnp.int32)
    return ti, tj, nq, len(steps)


def _attention(q, k, v_t, cum_col):
    bsz, pairs, _, s, _ = q.shape
    tq = min(ATTN_Q_TILE, s)
    tk = min(ATTN_K_TILE, s)
    assert tk % tq == 0
    ti, tj, nq, n_steps = _attention_steps(s, tq, tk)
    kern = functools.partial(_attn_kernel, tq=tq, tk=tk, n_diag=nq, n_steps=n_steps)
    grid_spec = pltpu.PrefetchScalarGridSpec(
        num_scalar_prefetch=2,
        grid=(bsz, pairs),
        in_specs=[pl.BlockSpec((1, 1, 2, s, LANES), lambda b, p, ti, tj: (b, p, 0, 0, 0)),
                  pl.BlockSpec((1, 1, s, LANES), lambda b, p, ti, tj: (b, p, 0, 0)),
                  pl.BlockSpec((1, 1, LANES, s), lambda b, p, ti, tj: (b, p, 0, 0)),
                  pl.BlockSpec((1, 1, s, 2), lambda b, p, ti, tj: (b, p, 0, 0))],
        out_specs=pl.BlockSpec((1, s, LANES), lambda b, p, ti, tj: (b, 0, p)),
        scratch_shapes=[pltpu.VMEM((2, 2, tk, tq), F32),
                        pltpu.VMEM((2, 2, tk, tq), BF16),
                        pltpu.VMEM((2, 2, 1, tq), F32),
                        pltpu.VMEM((nq, 2, 1, tq), F32),
                        pltpu.VMEM((nq, 2, 1, tq), F32),
                        pltpu.VMEM((nq, LANES, tq), F32)])
    return pl.pallas_call(
        kern,
        grid_spec=grid_spec,
        out_shape=jax.ShapeDtypeStruct((bsz, s, pairs * LANES), BF16),
        compiler_params=_params("arbitrary", "arbitrary"),
        name="forgetting_attention",
    )(ti, tj, q, k, v_t, cum_col)


def _merge_kernel(x_ref, mod_ref, gpre_ref, gpost_ref, wgate_ref, ys_ref, ya_ref,
                  wpa_ref, wpb_ref, wo_ref, o_ref, *, sub):
    b = pl.program_id(0)
    d = x_ref.shape[-1]
    for r0 in range(0, x_ref.shape[1], sub):
        rows = slice(r0, r0 + sub)
        x = x_ref[0, rows, :]
        h = _rmsnorm(x, gpre_ref[...]) * (1.0 + _mod_row(mod_ref, 1, b)) + _mod_row(mod_ref, 0, b)
        gates = jnp.dot(h.astype(BF16), wgate_ref[...], preferred_element_type=F32)
        pa = jnp.dot(ys_ref[0, rows, :], wpa_ref[...], preferred_element_type=F32)
        pb = jnp.dot(ya_ref[0, rows, :], wpb_ref[...], preferred_element_type=F32)
        merged = jax.nn.sigmoid(gates[:, :d]) * pa + jax.nn.sigmoid(gates[:, d:]) * pb
        y = jnp.dot(merged.astype(BF16), wo_ref[...], preferred_element_type=F32)
        o_ref[0, rows, :] = x + _mod_row(mod_ref, 2, b) * _rmsnorm(y, gpost_ref[...])


def _resident(a, layer=None):
    if layer is None:
        return pl.BlockSpec(a.shape, lambda b, i: (0,) * a.ndim, pipeline_mode=pl.Buffered(1))
    return pl.BlockSpec((None,) + a.shape[1:], lambda b, i: (layer,) + (0,) * (a.ndim - 1),
                        pipeline_mode=pl.Buffered(1))


def _merge(x, mod, g_pre, g_post, w_gate, y_ssm, y_att, w_pa, w_pb, w_o, layer):
    bsz, s, d = x.shape
    tm = min(2 * TOKEN_TILE, s)
    width = y_att.shape[-1]
    full = lambda a: pl.BlockSpec(a.shape, lambda b, i: (0,) * a.ndim)
    g_pre = g_pre.reshape(1, d)
    g_post = g_post.reshape(1, d)
    return pl.pallas_call(
        functools.partial(_merge_kernel, sub=min(TOKEN_TILE, tm)),
        grid=(bsz, s // tm),
        in_specs=[pl.BlockSpec((1, tm, d), lambda b, i: (b, i, 0)),
                  full(mod), full(g_pre), full(g_post), _resident(w_gate),
                  pl.BlockSpec((1, tm, width), lambda b, i: (b, i, 0)),
                  pl.BlockSpec((1, tm, width), lambda b, i: (b, i, 0)),
                  _resident(w_pa, layer), _resident(w_pb, layer), _resident(w_o, layer)],
        out_specs=pl.BlockSpec((1, tm, d), lambda b, i: (b, i, 0)),
        out_shape=jax.ShapeDtypeStruct(x.shape, F32),
        compiler_params=_params("arbitrary", "arbitrary"),
        name="gated_merge",
    )(x, mod, g_pre, g_post, w_gate, y_ssm, y_att, w_pa, w_pb, w_o)


def _ffn_chunks(d_ff):
    mxu = 2 * LANES
    n_tiles = d_ff // mxu
    first = (n_tiles + 1) // 2 * mxu
    return ((0, first), (first, d_ff)) if first < d_ff else ((0, d_ff),)


def _ffn_kernel(x_ref, mod_ref, gpre_ref, gpost_ref, wg_ref, wu_ref, wd_ref, o_ref, *, chunks, sub):
    b = pl.program_id(0)
    for r0 in range(0, x_ref.shape[1], sub):
        x = x_ref[0, r0:r0 + sub, :]
        h = _rmsnorm(x, gpre_ref[...]) * (1.0 + _mod_row(mod_ref, 4, b)) + _mod_row(mod_ref, 3, b)
        hb = h.astype(BF16)
        acc = None
        for c0, c1 in chunks:
            g = jnp.dot(hb, wg_ref[:, c0:c1], preferred_element_type=F32)
            up = jnp.dot(hb, wu_ref[:, c0:c1], preferred_element_type=F32)
            a = (g * jax.nn.sigmoid(g) * up).astype(BF16)
            part = jnp.dot(a, wd_ref[c0:c1, :], preferred_element_type=F32)
            acc = part if acc is None else acc + part
        o_ref[0, r0:r0 + sub, :] = x + _mod_row(mod_ref, 5, b) * _rmsnorm(acc, gpost_ref[...])


def _ffn(x, mod, g_pre, g_post, w_gate, w_up, w_down, layer):
    bsz, s, d = x.shape
    tm = min(2 * TOKEN_TILE, s)
    kern = functools.partial(_ffn_kernel, chunks=_ffn_chunks(w_gate.shape[-1]),
                             sub=min(TOKEN_TILE, tm))
    full = lambda a: pl.BlockSpec(a.shape, lambda b, i: (0,) * a.ndim)
    g_pre = g_pre.reshape(1, d)
    g_post = g_post.reshape(1, d)
    return pl.pallas_call(
        kern,
        grid=(bsz, s // tm),
        in_specs=[pl.BlockSpec((1, tm, d), lambda b, i: (b, i, 0)),
                  full(mod), full(g_pre), full(g_post),
                  _resident(w_gate, layer), _resident(w_up, layer), _resident(w_down, layer)],
        out_specs=pl.BlockSpec((1, tm, d), lambda b, i: (b, i, 0)),
        out_shape=jax.ShapeDtypeStruct(x.shape, F32),
        compiler_params=_params("arbitrary", "arbitrary"),
        name="swiglu_ffn",
    )(x, mod, g_pre, g_post, w_gate, w_up, w_down)


def kernel(x, c, w_ada, b_ada, g_pre_mix, g_post_mix, g_pre_ffn, g_post_ffn, w_in, lam_re, lam_im, log_dt, b_re, b_im, c_re, c_im, d_skip, w_glu, b_glu, b_f, w_pa, w_pb, w_o, w_ffn_gate, w_ffn_up, w_ffn_down):
    bsz, s, d = x.shape
    depth = w_in.shape[0]
    ssm_w = d_skip.shape[1]
    heads = b_f.shape[1]
    attn_w = heads * HEAD_DIM
    pairs = attn_w // LANES
    main_w = ssm_w + 3 * attn_w
    assert bsz == SUBLANES, "the S5 scan keeps the batch on the sublane axis"

    mod_all = _modulation(c, w_ada, b_ada)
    w_in_b = w_in.astype(BF16)
    w_pa_b, w_pb_b, w_o_b = w_pa.astype(BF16), w_pb.astype(BF16), w_o.astype(BF16)
    w_ffn = (w_ffn_gate.astype(BF16), w_ffn_up.astype(BF16), w_ffn_down.astype(BF16))
    for l in range(depth):
        mod = mod_all[l]
        uqk_w = ssm_w + 2 * attn_w
        w_main = w_in_b[l, :, :uqk_w]
        w_vf_t = w_in_b[l, :, uqk_w:main_w + heads].T
        w_gate = w_in_b[l, :, main_w + heads:]

        u, q, k, v_t, f_t = _in_projection(x, mod, g_pre_mix[l], w_main, w_vf_t, ssm_w, attn_w)

        cum_col = _forget_cumsum(f_t, b_f[l])
        y_att = _attention(q, k, v_t, cum_col)

        bmat, a_re, a_im, cmat = _s5_matrices(lam_re[l], lam_im[l], log_dt[l], b_re[l], b_im[l],
                                              c_re[l], c_im[l], bsz)
        y_ssm = _ssm_branch(u, bmat, a_re, a_im, cmat,
                            d_skip[l], w_glu[l].astype(BF16), b_glu[l])

        x = _merge(x, mod, g_pre_mix[l], g_post_mix[l], w_gate, y_ssm, y_att,
                   w_pa_b, w_pb_b, w_o_b, l)
        x = _ffn(x, mod, g_pre_ffn[l], g_post_ffn[l], *w_ffn, l)
    return x
```

```python
import functools
import math

import jax
import jax.numpy as jnp
from jax import lax
from jax.experimental import pallas as pl
from jax.experimental.pallas import tpu as pltpu

F32 = jnp.float32
BF16 = jnp.bfloat16

SSM_GROUP_CH = 16
SSM_STATE = 64
HEAD_DIM = 64
N_MOD = 6
RMS_EPS = 1e-6
EIG_CLIP = 1e-4
LOG2E = math.log2(math.e)

LANES = 128
SUBLANES = 8
SLAB = 2 * LANES
VMEM_LIMIT = 56 * 1024 * 1024

TOKEN_TILE = 512
SCAN_STEPS = 64
ATTN_Q_TILE = 512
ATTN_K_WIDE = 1024
SCAN_GROUP = 4
SCAN_UNROLL = 4


def _params(*sem):
    return pltpu.CompilerParams(dimension_semantics=sem, vmem_limit_bytes=VMEM_LIMIT)


def _rmsnorm(x, g):
    r = lax.rsqrt(jnp.mean(x * x, axis=-1, keepdims=True) + RMS_EPS)
    return x * r * g


def _mod_row(mod_ref, j, b):
    return mod_ref[j, pl.ds(b, 1), :]


def _mod_kernel(c_ref, w_ref, b_ref, o_ref):
    c = c_ref[...]
    cond = c * jax.nn.sigmoid(c)
    o_ref[0, 0] = jnp.dot(cond.astype(BF16), w_ref[0].astype(BF16),
                          preferred_element_type=F32) + b_ref[0, 0]


def _modulation(c, w_ada, b_ada):
    depth, d, nd = w_ada.shape
    bsz = c.shape[0]
    n = nd // d
    return pl.pallas_call(
        _mod_kernel,
        grid=(depth, n),
        in_specs=[pl.BlockSpec((bsz, d), lambda l, j: (0, 0)),
                  pl.BlockSpec((1, d, d), lambda l, j: (l, 0, j)),
                  pl.BlockSpec((1, 1, 1, d), lambda l, j: (l, j, 0, 0))],
        out_specs=pl.BlockSpec((1, 1, bsz, d), lambda l, j: (l, j, 0, 0)),
        out_shape=jax.ShapeDtypeStruct((depth, n, bsz, d), F32),
        compiler_params=_params("arbitrary", "arbitrary"),
        name="adaln_mod",
    )(c, w_ada, b_ada.reshape(depth, n, 1, d))


def _in_kernel(x_ref, mod_ref, g_ref, w_ref, wvf_ref, u_ref, q_ref, k_ref, vt_ref, f_ref,
               *, ssm_w, attn_w):
    b = pl.program_id(0)
    x = x_ref[0]
    h = _rmsnorm(x, g_ref[...]) * (1.0 + _mod_row(mod_ref, 1, b)) + _mod_row(mod_ref, 0, b)
    hb = h.astype(BF16)
    proj = jnp.dot(hb, w_ref[...], preferred_element_type=F32)
    u_ref[0] = proj[:, :ssm_w]
    scale = HEAD_DIM ** -0.5 * LOG2E
    proj_t = lax.dot_general(wvf_ref[...], hb, (((1,), (1,)), ((), ())),
                             preferred_element_type=F32)
    lane = lax.broadcasted_iota(jnp.int32, (x.shape[0], LANES), 1)
    for p in range(attn_w // LANES):
        c0 = ssm_w + p * LANES
        q_pair = proj[:, c0:c0 + LANES] * scale
        q_ref[0, p, 0] = jnp.where(lane < HEAD_DIM, q_pair, 0.0).astype(BF16)
        q_ref[0, p, 1] = jnp.where(lane >= HEAD_DIM, q_pair, 0.0).astype(BF16)
        k_ref[0, p] = proj[:, c0 + attn_w:c0 + attn_w + LANES].astype(BF16)
        vt_ref[0, p] = proj_t[p * LANES:(p + 1) * LANES, :].astype(BF16)
    f_ref[0] = proj_t[attn_w:, :]


def _in_projection(x, mod, g, w_main, w_vf_t, ssm_w, attn_w):
    bsz, s, d = x.shape
    tm = min(TOKEN_TILE, s)
    pairs = attn_w // LANES
    heads = w_vf_t.shape[0] - attn_w
    kern = functools.partial(_in_kernel, ssm_w=ssm_w, attn_w=attn_w)
    qk_shape = jax.ShapeDtypeStruct((bsz, pairs, s, LANES), BF16)
    qk_spec = pl.BlockSpec((1, pairs, tm, LANES), lambda b, i: (b, 0, i, 0))
    return pl.pallas_call(
        kern,
        grid=(bsz, s // tm),
        in_specs=[pl.BlockSpec((1, tm, d), lambda b, i: (b, i, 0)),
                  pl.BlockSpec(mod.shape, lambda b, i: (0, 0, 0)),
                  pl.BlockSpec((1, d), lambda b, i: (0, 0)),
                  pl.BlockSpec(w_main.shape, lambda b, i: (0, 0)),
                  pl.BlockSpec(w_vf_t.shape, lambda b, i: (0, 0))],
        out_specs=[pl.BlockSpec((1, tm, ssm_w), lambda b, i: (b, i, 0)),
                   pl.BlockSpec((1, pairs, 2, tm, LANES), lambda b, i: (b, 0, 0, i, 0)),
                   qk_spec,
                   pl.BlockSpec((1, pairs, LANES, tm), lambda b, i: (b, 0, 0, i)),
                   pl.BlockSpec((1, heads, tm), lambda b, i: (b, 0, i))],
        out_shape=[jax.ShapeDtypeStruct((bsz, s, ssm_w), F32),
                   jax.ShapeDtypeStruct((bsz, pairs, 2, s, LANES), BF16),
                   qk_shape,
                   jax.ShapeDtypeStruct((bsz, pairs, LANES, s), BF16),
                   jax.ShapeDtypeStruct((bsz, heads, s), F32)],
        compiler_params=_params("arbitrary", "arbitrary"),
        name="in_projection",
    )(x, mod, g.reshape(1, d), w_main, w_vf_t)


def _cum_kernel(f_ref, bf_ref, o_ref):
    x = jax.nn.log_sigmoid(f_ref[0] + bf_ref[...])
    s = x.shape[1]
    lane = lax.broadcasted_iota(jnp.int32, x.shape, 1)
    shift = 1
    while shift < s:
        x = x + jnp.where(lane >= shift, pltpu.roll(x, shift, axis=1), 0.0)
        shift *= 2
    cum_t = (x * LOG2E).T
    for p in range(o_ref.shape[1]):
        o_ref[0, p] = cum_t[:, 2 * p:2 * p + 2]


def _forget_cumsum(f_t, b_f):
    bsz, heads, s = f_t.shape
    pairs = heads // 2
    return pl.pallas_call(
        _cum_kernel,
        grid=(bsz,),
        in_specs=[pl.BlockSpec((1, heads, s), lambda b: (b, 0, 0)),
                  pl.BlockSpec((heads, 1), lambda b: (0, 0))],
        out_specs=pl.BlockSpec((1, pairs, s, 2), lambda b: (b, 0, 0, 0)),
        out_shape=jax.ShapeDtypeStruct((bsz, pairs, s, 2), F32),
        compiler_params=_params("arbitrary"),
        name="forget_cumsum",
    )(f_t, b_f.reshape(heads, 1))


def _ssm_kernel(u_ref, bm_ref, are_ref, aim_ref, cm_ref, dsk_ref, wg_ref, bg_ref, o_ref,
                buf, st, *, steps, n_slabs):
    @pl.when(pl.program_id(0) == 0)
    def _():
        st[...] = jnp.zeros_like(st)

    bsz, _, width = u_ref.shape
    rows = steps * bsz
    u = jnp.swapaxes(u_ref[...], 0, 1).reshape(rows, width)
    ub = u.astype(BF16)
    slabs_per_block = LANES // (2 * SSM_GROUP_CH)
    for j in range(n_slabs):
        m = j // slabs_per_block
        buf[j] = jnp.dot(ub[:, m * LANES:(m + 1) * LANES], bm_ref[j],
                         preferred_element_type=F32)

    for first in range(0, n_slabs, SCAN_GROUP):
        slabs = list(range(first, first + SCAN_GROUP))
        coefs = [(are_ref[j], aim_ref[j]) for j in slabs]
        init = tuple((st[j, :, :LANES], st[j, :, LANES:]) for j in slabs)

        def body(t, carry, slabs=slabs, coefs=coefs):
            row = t * SUBLANES
            new = []
            for (re, im), (are, aim), j in zip(carry, coefs, slabs):
                bre = buf[j, pl.ds(row, SUBLANES), :LANES]
                bim = buf[j, pl.ds(row, SUBLANES), LANES:]
                nre = are * re - aim * im + bre
                nim = are * im + aim * re + bim
                buf[j, pl.ds(row, SUBLANES), :LANES] = nre
                buf[j, pl.ds(row, SUBLANES), LANES:] = nim
                new.append((nre, nim))
            return tuple(new)

        fin = init
        for t in range(steps):
            fin = body(t, fin)
        for (re, im), j in zip(fin, slabs):
            st[j, :, :LANES] = re
            st[j, :, LANES:] = im

    ys = []
    for m in range(n_slabs // slabs_per_block):
        acc = None
        for j in range(m * slabs_per_block, (m + 1) * slabs_per_block):
            part = jnp.dot(buf[j].astype(BF16), cm_ref[j], preferred_element_type=F32)
            acc = part if acc is None else acc + part
        ys.append(acc)
    y = jnp.concatenate(ys, axis=1) + dsk_ref[...] * u
    z = jax.nn.gelu(y, approximate=True)
    gate = jax.nn.sigmoid(
        jnp.dot(z.astype(BF16), wg_ref[...], preferred_element_type=F32) + bg_ref[...])
    out = (z * gate).reshape(steps, bsz, width)
    o_ref[...] = jnp.swapaxes(out, 0, 1).astype(BF16)


def _ssm_branch(u, bmat, a_re, a_im, cmat, d_skip, w_glu, b_glu):
    bsz, s, width = u.shape
    steps = min(SCAN_STEPS, s)
    rows = steps * bsz
    n_slabs = bmat.shape[0]
    kern = functools.partial(_ssm_kernel, steps=steps, n_slabs=n_slabs)
    full = lambda a: pl.BlockSpec(a.shape, lambda i: (0,) * a.ndim)
    d_skip = d_skip.reshape(1, width)
    b_glu = b_glu.reshape(1, width)
    return pl.pallas_call(
        kern,
        grid=(s // steps,),
        in_specs=[pl.BlockSpec((bsz, steps, width), lambda i: (0, i, 0)),
                  full(bmat), full(a_re), full(a_im), full(cmat),
                  full(d_skip), full(w_glu), full(b_glu)],
        out_specs=pl.BlockSpec((bsz, steps, width), lambda i: (0, i, 0)),
        out_shape=jax.ShapeDtypeStruct((bsz, s, width), BF16),
        scratch_shapes=[pltpu.VMEM((n_slabs, rows, SLAB), F32),
                        pltpu.VMEM((n_slabs, bsz, SLAB), F32)],
        compiler_params=_params("arbitrary"),
        name="s5_branch",
    )(u, bmat, a_re, a_im, cmat, d_skip, w_glu, b_glu)


def _s5_matrices(lam_re, lam_im, log_dt, b_re, b_im, c_re, c_im, bsz):
    g, p = lam_re.shape
    h = b_re.shape[-1]
    lr = jnp.minimum(lam_re, -EIG_CLIP)
    li = lam_im
    dt = jnp.exp(log_dt)[:, None]
    mag = jnp.exp(lr * dt)
    a_re = mag * jnp.cos(li * dt)
    a_im = mag * jnp.sin(li * dt)
    den = lr * lr + li * li
    f_re = ((a_re - 1.0) * lr + a_im * li) / den
    f_im = (a_im * lr - (a_re - 1.0) * li) / den
    bb_re = f_re[..., None] * b_re - f_im[..., None] * b_im
    bb_im = f_re[..., None] * b_im + f_im[..., None] * b_re
    n_slabs = g // 2
    per_block = LANES // (2 * h)
    n_blocks = n_slabs // per_block
    eye_s = jnp.eye(per_block, dtype=F32)
    eye_g = jnp.eye(2, dtype=F32)

    def expand(w_hp, spec):
        w = w_hp.reshape(n_blocks, per_block, 2, h, p)
        return jnp.einsum(spec, w, eye_s, eye_g)

    b_spec = "mjghp,ja,gk->mjaghkp"
    bmat = jnp.concatenate(
        [expand(bb.transpose(0, 2, 1), b_spec).reshape(n_slabs, LANES, LANES)
         for bb in (bb_re, bb_im)], axis=2)
    c_spec = "mjghp,ja,gk->mjkpagh"
    cmat = jnp.concatenate(
        [expand(cc, c_spec).reshape(n_slabs, LANES, LANES) for cc in (c_re, -c_im)], axis=1)
    a_re_s = a_re.reshape(n_slabs, 1, 2 * p)
    a_im_s = a_im.reshape(n_slabs, 1, 2 * p)
    a_re_s = jnp.broadcast_to(a_re_s, (n_slabs, bsz, 2 * p))
    a_im_s = jnp.broadcast_to(a_im_s, (n_slabs, bsz, 2 * p))
    return bmat.astype(BF16), a_re_s, a_im_s, cmat.astype(BF16)


def _attn_kernel(ti_ref, tj_ref, q_ref, k_ref, vt_ref, ccol_ref, o_ref,
                 s_buf, p_buf, a_buf, m_sc, l_sc, acc_sc, *, tq, tk, n_diag, kinds):
    m_sc[...] = jnp.full_like(m_sc, -jnp.inf)
    l_sc[...] = jnp.zeros_like(l_sc)
    acc_sc[...] = jnp.zeros_like(acc_sc)
    p_buf[1] = jnp.zeros(p_buf.shape[1:], BF16)
    a_buf[1] = jnp.ones(a_buf.shape[1:], F32)

    n_steps = len(kinds)

    def kind_of(n):
        return kinds[min(max(n, 0), n_steps - 1)]

    def tile_of(n):
        i = ti_ref[n]
        return i, pl.multiple_of(i * tq, tq), pl.multiple_of(tj_ref[n] * tk, tk)

    def scores(n, slot, kind):
        masked, width = kind
        _, qs, ks = tile_of(n)
        kt = k_ref[0, 0, pl.ds(ks, width), :]
        for hh in range(2):
            ck = ccol_ref[0, 0, pl.ds(ks, width), hh:hh + 1]
            s = lax.dot_general(kt, q_ref[0, 0, hh, pl.ds(qs, tq), :], (((1,), (1,)), ((), ())),
                                preferred_element_type=F32) - ck
            if masked:
                kpos = ks + lax.broadcasted_iota(jnp.int32, s.shape, 0)
                qpos = qs + lax.broadcasted_iota(jnp.int32, s.shape, 1)
                s = jnp.where(kpos <= qpos, s, -jnp.inf)
            s_buf[slot, hh, :width, :] = s

    def softmax(n, slot, kind):
        _, width = kind
        i, _, _ = tile_of(n)
        for hh in range(2):
            m_old = m_sc[i, hh]
            m_new = jnp.maximum(m_old, jnp.max(s_buf[slot, hh, :width, :], axis=0, keepdims=True))
            m_sc[i, hh] = m_new
            alpha = jnp.exp2(m_old - m_new)
            pexp = jnp.exp2(s_buf[slot, hh, :width, :] - m_new)
            l_sc[i, hh] = alpha * l_sc[i, hh] + jnp.sum(pexp, axis=0, keepdims=True)
            a_buf[slot, hh] = alpha
            p_buf[slot, hh, :width, :] = pexp.astype(BF16)

    def values(n, slot, kind):
        _, width = kind
        i, _, ks = tile_of(n)
        vt = vt_ref[0, 0, :, pl.ds(ks, width)]
        for hh in range(2):
            rows = slice(hh * HEAD_DIM, (hh + 1) * HEAD_DIM)
            acc_sc[i, rows, :] = a_buf[slot, hh] * acc_sc[i, rows, :] + jnp.dot(
                vt[rows, :], p_buf[slot, hh, :width, :], preferred_element_type=F32)

    def step_pair(window):
        k_prev, k_even, k_odd, k_next = window

        def body(t, carry):
            n = 2 * t
            prev = max(n - 1, 0) if isinstance(n, int) else jnp.maximum(n - 1, 0)
            scores(n + 1, 1, k_odd)
            softmax(n, 0, k_even)
            values(prev, 1, k_prev)
            scores(n + 2, 0, k_next)
            softmax(n + 1, 1, k_odd)
            values(n, 0, k_even)
            return carry
        return body

    windows = [tuple(kind_of(2 * t + d) for d in (-1, 0, 1, 2)) for t in range(n_steps // 2)]
    scores(0, 0, kind_of(0))
    t0 = 0
    while t0 < len(windows):
        t1 = t0
        while t1 < len(windows) and windows[t1] == windows[t0]:
            t1 += 1
        if t1 - t0 > 1:
            lax.fori_loop(t0, t1, step_pair(windows[t0]), 0)
        else:
            step_pair(windows[t0])(t0, 0)
        t0 = t1
    values(n_steps - 1, 1, kind_of(n_steps - 1))

    def finalize(i, carry):
        out_t = jnp.concatenate([acc_sc[i, :HEAD_DIM, :] / l_sc[i, 0],
                                 acc_sc[i, HEAD_DIM:, :] / l_sc[i, 1]], axis=0)
        o_ref[0, pl.ds(pl.multiple_of(i * tq, tq), tq), :] = out_t.T.astype(BF16)
        return carry

    lax.fori_loop(0, n_diag, finalize, 0)


def _attention_steps(s, tile, wide):
    nq = s // tile
    per_wide = wide // tile
    diag = [(i, i) for i in range(nq)]
    narrow = [(i, j) for i in range(nq) for j in range(i // per_wide * per_wide, i)]
    broad = [(i, w * per_wide) for i in range(nq) for w in range(i // per_wide)]
    groups = [(diag, (True, tile)), (narrow, (False, tile)), (broad, (False, wide))]
    assert all(len(g) % 2 == 0 for g, _ in groups), "the pipeline advances two steps at a time"
    steps = [st for g, _ in groups for st in g]
    kinds = tuple(kind for g, kind in groups for _ in g)
    padded = steps + [steps[-1]] * 2
    ti = jnp.asarray([i for i, _ in padded], jnp.int32)
    tj = jnp.asarray([j for _, j in padded], jnp.int32)
    return ti, tj, nq, kinds


def _attention(q, k, v_t, cum_col):
    bsz, pairs, _, s, _ = q.shape
    tq = min(ATTN_Q_TILE, s)
    wide = min(ATTN_K_WIDE, s)
    ti, tj, nq, kinds = _attention_steps(s, tq, wide)
    tk = max(width for _, width in kinds)
    kern = functools.partial(_attn_kernel, tq=tq, tk=tq, n_diag=nq, kinds=kinds)
    grid_spec = pltpu.PrefetchScalarGridSpec(
        num_scalar_prefetch=2,
        grid=(bsz, pairs),
        in_specs=[pl.BlockSpec((1, 1, 2, s, LANES), lambda b, p, ti, tj: (b, p, 0, 0, 0)),
                  pl.BlockSpec((1, 1, s, LANES), lambda b, p, ti, tj: (b, p, 0, 0)),
                  pl.BlockSpec((1, 1, LANES, s), lambda b, p, ti, tj: (b, p, 0, 0)),
                  pl.BlockSpec((1, 1, s, 2), lambda b, p, ti, tj: (b, p, 0, 0))],
        out_specs=pl.BlockSpec((1, s, LANES), lambda b, p, ti, tj: (b, 0, p)),
        scratch_shapes=[pltpu.VMEM((2, 2, tk, tq), F32),
                        pltpu.VMEM((2, 2, tk, tq), BF16),
                        pltpu.VMEM((2, 2, 1, tq), F32),
                        pltpu.VMEM((nq, 2, 1, tq), F32),
                        pltpu.VMEM((nq, 2, 1, tq), F32),
                        pltpu.VMEM((nq, LANES, tq), F32)])
    return pl.pallas_call(
        kern,
        grid_spec=grid_spec,
        out_shape=jax.ShapeDtypeStruct((bsz, s, pairs * LANES), BF16),
        compiler_params=_params("arbitrary", "arbitrary"),
        name="forgetting_attention",
    )(ti, tj, q, k, v_t, cum_col)


def _merge_kernel(x_ref, mod_ref, gpre_ref, gpost_ref, wgate_ref, ys_ref, ya_ref,
                  wpa_ref, wpb_ref, wo_ref, o_ref, *, sub):
    b = pl.program_id(0)
    d = x_ref.shape[-1]
    for r0 in range(0, x_ref.shape[1], sub):
        rows = slice(r0, r0 + sub)
        x = x_ref[0, rows, :]
        h = _rmsnorm(x, gpre_ref[...]) * (1.0 + _mod_row(mod_ref, 1, b)) + _mod_row(mod_ref, 0, b)
        gates = jnp.dot(h.astype(BF16), wgate_ref[...], preferred_element_type=F32)
        pa = jnp.dot(ys_ref[0, rows, :], wpa_ref[...], preferred_element_type=F32)
        pb = jnp.dot(ya_ref[0, rows, :], wpb_ref[...], preferred_element_type=F32)
        merged = jax.nn.sigmoid(gates[:, :d]) * pa + jax.nn.sigmoid(gates[:, d:]) * pb
        y = jnp.dot(merged.astype(BF16), wo_ref[...], preferred_element_type=F32)
        o_ref[0, rows, :] = x + _mod_row(mod_ref, 2, b) * _rmsnorm(y, gpost_ref[...])


def _resident(a, layer=None):
    if layer is None:
        return pl.BlockSpec(a.shape, lambda b, i: (0,) * a.ndim, pipeline_mode=pl.Buffered(1))
    return pl.BlockSpec((None,) + a.shape[1:], lambda b, i: (layer,) + (0,) * (a.ndim - 1),
                        pipeline_mode=pl.Buffered(1))


def _merge(x, mod, g_pre, g_post, w_gate, y_ssm, y_att, w_pa, w_pb, w_o, layer):
    bsz, s, d = x.shape
    tm = min(2 * TOKEN_TILE, s)
    width = y_att.shape[-1]
    full = lambda a: pl.BlockSpec(a.shape, lambda b, i: (0,) * a.ndim)
    g_pre = g_pre.reshape(1, d)
    g_post = g_post.reshape(1, d)
    return pl.pallas_call(
        functools.partial(_merge_kernel, sub=min(TOKEN_TILE, tm)),
        grid=(bsz, s // tm),
        in_specs=[pl.BlockSpec((1, tm, d), lambda b, i: (b, i, 0)),
                  full(mod), full(g_pre), full(g_post), _resident(w_gate),
                  pl.BlockSpec((1, tm, width), lambda b, i: (b, i, 0)),
                  pl.BlockSpec((1, tm, width), lambda b, i: (b, i, 0)),
                  _resident(w_pa, layer), _resident(w_pb, layer), _resident(w_o, layer)],
        out_specs=pl.BlockSpec((1, tm, d), lambda b, i: (b, i, 0)),
        out_shape=jax.ShapeDtypeStruct(x.shape, F32),
        compiler_params=_params("arbitrary", "arbitrary"),
        name="gated_merge",
    )(x, mod, g_pre, g_post, w_gate, y_ssm, y_att, w_pa, w_pb, w_o)


def _ffn_chunks(d_ff):
    mxu = 2 * LANES
    n_tiles = d_ff // mxu
    first = (n_tiles + 1) // 2 * mxu
    return ((0, first), (first, d_ff)) if first < d_ff else ((0, d_ff),)


def _ffn_kernel(x_ref, mod_ref, gpre_ref, gpost_ref, wg_ref, wu_ref, wd_ref, o_ref, *, chunks, sub):
    b = pl.program_id(0)
    for r0 in range(0, x_ref.shape[1], sub):
        x = x_ref[0, r0:r0 + sub, :]
        h = _rmsnorm(x, gpre_ref[...]) * (1.0 + _mod_row(mod_ref, 4, b)) + _mod_row(mod_ref, 3, b)
        hb = h.astype(BF16)
        acc = None
        for c0, c1 in chunks:
            g = jnp.dot(hb, wg_ref[:, c0:c1], preferred_element_type=F32)
            up = jnp.dot(hb, wu_ref[:, c0:c1], preferred_element_type=F32)
            a = (g * jax.nn.sigmoid(g) * up).astype(BF16)
            part = jnp.dot(a, wd_ref[c0:c1, :], preferred_element_type=F32)
            acc = part if acc is None else acc + part
        o_ref[0, r0:r0 + sub, :] = x + _mod_row(mod_ref, 5, b) * _rmsnorm(acc, gpost_ref[...])


def _ffn(x, mod, g_pre, g_post, w_gate, w_up, w_down, layer):
    bsz, s, d = x.shape
    tm = min(2 * TOKEN_TILE, s)
    kern = functools.partial(_ffn_kernel, chunks=_ffn_chunks(w_gate.shape[-1]),
                             sub=min(TOKEN_TILE, tm))
    full = lambda a: pl.BlockSpec(a.shape, lambda b, i: (0,) * a.ndim)
    g_pre = g_pre.reshape(1, d)
    g_post = g_post.reshape(1, d)
    return pl.pallas_call(
        kern,
        grid=(bsz, s // tm),
        in_specs=[pl.BlockSpec((1, tm, d), lambda b, i: (b, i, 0)),
                  full(mod), full(g_pre), full(g_post),
                  _resident(w_gate, layer), _resident(w_up, layer), _resident(w_down, layer)],
        out_specs=pl.BlockSpec((1, tm, d), lambda b, i: (b, i, 0)),
        out_shape=jax.ShapeDtypeStruct(x.shape, F32),
        compiler_params=_params("arbitrary", "arbitrary"),
        name="swiglu_ffn",
    )(x, mod, g_pre, g_post, w_gate, w_up, w_down)


def kernel(x, c, w_ada, b_ada, g_pre_mix, g_post_mix, g_pre_ffn, g_post_ffn, w_in, lam_re, lam_im, log_dt, b_re, b_im, c_re, c_im, d_skip, w_glu, b_glu, b_f, w_pa, w_pb, w_o, w_ffn_gate, w_ffn_up, w_ffn_down):
    bsz, s, d = x.shape
    depth = w_in.shape[0]
    ssm_w = d_skip.shape[1]
    heads = b_f.shape[1]
    attn_w = heads * HEAD_DIM
    pairs = attn_w // LANES
    main_w = ssm_w + 3 * attn_w
    assert bsz == SUBLANES, "the S5 scan keeps the batch on the sublane axis"

    mod_all = _modulation(c, w_ada, b_ada)
    w_in_b = w_in.astype(BF16)
    w_pa_b, w_pb_b, w_o_b = w_pa.astype(BF16), w_pb.astype(BF16), w_o.astype(BF16)
    w_ffn = (w_ffn_gate.astype(BF16), w_ffn_up.astype(BF16), w_ffn_down.astype(BF16))
    for l in range(depth):
        mod = mod_all[l]
        uqk_w = ssm_w + 2 * attn_w
        w_main = w_in_b[l, :, :uqk_w]
        w_vf_t = w_in_b[l, :, uqk_w:main_w + heads].T
        w_gate = w_in_b[l, :, main_w + heads:]

        u, q, k, v_t, f_t = _in_projection(x, mod, g_pre_mix[l], w_main, w_vf_t, ssm_w, attn_w)

        cum_col = _forget_cumsum(f_t, b_f[l])
        y_att = _attention(q, k, v_t, cum_col)

        bmat, a_re, a_im, cmat = _s5_matrices(lam_re[l], lam_im[l], log_dt[l], b_re[l], b_im[l],
                                              c_re[l], c_im[l], bsz)
        y_ssm = _ssm_branch(u, bmat, a_re, a_im, cmat,
                            d_skip[l], w_glu[l].astype(BF16), b_glu[l])

        x = _merge(x, mod, g_pre_mix[l], g_post_mix[l], w_gate, y_ssm, y_att,
                   w_pa_b, w_pb_b, w_o_b, l)
        x = _ffn(x, mod, g_pre_ffn[l], g_post_ffn[l], *w_ffn, l)
    return x
```

```python
import functools
import math

import jax
import jax.numpy as jnp
from jax import lax
from jax.experimental import pallas as pl
from jax.experimental.pallas import tpu as pltpu

F32 = jnp.float32
BF16 = jnp.bfloat16

SSM_GROUP_CH = 16
SSM_STATE = 64
HEAD_DIM = 64
N_MOD = 6
RMS_EPS = 1e-6
EIG_CLIP = 1e-4
LOG2E = math.log2(math.e)

LANES = 128
SUBLANES = 8
SLAB = 2 * LANES
VMEM_LIMIT = 56 * 1024 * 1024

TOKEN_TILE = 512
SCAN_STEPS = 64
ATTN_Q_TILE = 512
ATTN_K_WIDE = 1024
SCAN_GROUP = 4
SCAN_UNROLL = 4


def _params(*sem):
    return pltpu.CompilerParams(dimension_semantics=sem, vmem_limit_bytes=VMEM_LIMIT)


def _rmsnorm(x, g):
    r = lax.rsqrt(jnp.mean(x * x, axis=-1, keepdims=True) + RMS_EPS)
    return x * r * g


def _mod_row(mod_ref, j, b):
    return mod_ref[j, pl.ds(b, 1), :]


def _mod_kernel(c_ref, w_ref, b_ref, o_ref):
    c = c_ref[...]
    cond = c * jax.nn.sigmoid(c)
    o_ref[0, 0] = jnp.dot(cond.astype(BF16), w_ref[0].astype(BF16),
                          preferred_element_type=F32) + b_ref[0, 0]


def _modulation(c, w_ada, b_ada):
    depth, d, nd = w_ada.shape
    bsz = c.shape[0]
    n = nd // d
    return pl.pallas_call(
        _mod_kernel,
        grid=(depth, n),
        in_specs=[pl.BlockSpec((bsz, d), lambda l, j: (0, 0)),
                  pl.BlockSpec((1, d, d), lambda l, j: (l, 0, j)),
                  pl.BlockSpec((1, 1, 1, d), lambda l, j: (l, j, 0, 0))],
        out_specs=pl.BlockSpec((1, 1, bsz, d), lambda l, j: (l, j, 0, 0)),
        out_shape=jax.ShapeDtypeStruct((depth, n, bsz, d), F32),
        compiler_params=_params("arbitrary", "arbitrary"),
        name="adaln_mod",
    )(c, w_ada, b_ada.reshape(depth, n, 1, d))


def _in_kernel(x_ref, mod_ref, g_ref, w_ref, wvf_ref, u_ref, q_ref, k_ref, vt_ref, f_ref,
               *, ssm_w, attn_w):
    b = pl.program_id(0)
    x = x_ref[0]
    h = _rmsnorm(x, g_ref[...]) * (1.0 + _mod_row(mod_ref, 1, b)) + _mod_row(mod_ref, 0, b)
    hb = h.astype(BF16)
    proj = jnp.dot(hb, w_ref[...], preferred_element_type=F32)
    u_ref[0] = proj[:, :ssm_w]
    scale = HEAD_DIM ** -0.5 * LOG2E
    proj_t = lax.dot_general(wvf_ref[...], hb, (((1,), (1,)), ((), ())),
                             preferred_element_type=F32)
    lane = lax.broadcasted_iota(jnp.int32, (x.shape[0], LANES), 1)
    for p in range(attn_w // LANES):
        c0 = ssm_w + p * LANES
        q_pair = proj[:, c0:c0 + LANES] * scale
        q_ref[0, p, 0] = jnp.where(lane < HEAD_DIM, q_pair, 0.0).astype(BF16)
        q_ref[0, p, 1] = jnp.where(lane >= HEAD_DIM, q_pair, 0.0).astype(BF16)
        k_ref[0, p] = proj[:, c0 + attn_w:c0 + attn_w + LANES].astype(BF16)
        vt_ref[0, p] = proj_t[p * LANES:(p + 1) * LANES, :].astype(BF16)
    f_ref[0] = proj_t[attn_w:, :]


def _in_projection(x, mod, g, w_main, w_vf_t, ssm_w, attn_w):
    bsz, s, d = x.shape
    tm = min(TOKEN_TILE, s)
    pairs = attn_w // LANES
    heads = w_vf_t.shape[0] - attn_w
    kern = functools.partial(_in_kernel, ssm_w=ssm_w, attn_w=attn_w)
    qk_shape = jax.ShapeDtypeStruct((bsz, pairs, s, LANES), BF16)
    qk_spec = pl.BlockSpec((1, pairs, tm, LANES), lambda b, i: (b, 0, i, 0))
    return pl.pallas_call(
        kern,
        grid=(bsz, s // tm),
        in_specs=[pl.BlockSpec((1, tm, d), lambda b, i: (b, i, 0)),
                  pl.BlockSpec(mod.shape, lambda b, i: (0, 0, 0)),
                  pl.BlockSpec((1, d), lambda b, i: (0, 0)),
                  pl.BlockSpec(w_main.shape, lambda b, i: (0, 0)),
                  pl.BlockSpec(w_vf_t.shape, lambda b, i: (0, 0))],
        out_specs=[pl.BlockSpec((1, tm, ssm_w), lambda b, i: (b, i, 0)),
                   pl.BlockSpec((1, pairs, 2, tm, LANES), lambda b, i: (b, 0, 0, i, 0)),
                   qk_spec,
                   pl.BlockSpec((1, pairs, LANES, tm), lambda b, i: (b, 0, 0, i)),
                   pl.BlockSpec((1, heads, tm), lambda b, i: (b, 0, i))],
        out_shape=[jax.ShapeDtypeStruct((bsz, s, ssm_w), F32),
                   jax.ShapeDtypeStruct((bsz, pairs, 2, s, LANES), BF16),
                   qk_shape,
                   jax.ShapeDtypeStruct((bsz, pairs, LANES, s), BF16),
                   jax.ShapeDtypeStruct((bsz, heads, s), F32)],
        compiler_params=_params("arbitrary", "arbitrary"),
        name="in_projection",
    )(x, mod, g.reshape(1, d), w_main, w_vf_t)


def _cum_kernel(f_ref, bf_ref, o_ref):
    x = jax.nn.log_sigmoid(f_ref[0] + bf_ref[...])
    s = x.shape[1]
    lane = lax.broadcasted_iota(jnp.int32, x.shape, 1)
    shift = 1
    while shift < s:
        x = x + jnp.where(lane >= shift, pltpu.roll(x, shift, axis=1), 0.0)
        shift *= 2
    cum_t = (x * LOG2E).T
    for p in range(o_ref.shape[1]):
        o_ref[0, p] = cum_t[:, 2 * p:2 * p + 2]


def _forget_cumsum(f_t, b_f):
    bsz, heads, s = f_t.shape
    pairs = heads // 2
    return pl.pallas_call(
        _cum_kernel,
        grid=(bsz,),
        in_specs=[pl.BlockSpec((1, heads, s), lambda b: (b, 0, 0)),
                  pl.BlockSpec((heads, 1), lambda b: (0, 0))],
        out_specs=pl.BlockSpec((1, pairs, s, 2), lambda b: (b, 0, 0, 0)),
        out_shape=jax.ShapeDtypeStruct((bsz, pairs, s, 2), F32),
        compiler_params=_params("arbitrary"),
        name="forget_cumsum",
    )(f_t, b_f.reshape(heads, 1))


def _ssm_kernel(u_ref, bm_ref, are_ref, aim_ref, cm_ref, dsk_ref, wg_ref, bg_ref, o_ref,
                buf, st, *, steps, n_slabs):
    @pl.when(pl.program_id(0) == 0)
    def _():
        st[...] = jnp.zeros_like(st)

    bsz, _, width = u_ref.shape
    rows = steps * bsz
    u = jnp.swapaxes(u_ref[...], 0, 1).reshape(rows, width)
    ub = u.astype(BF16)
    slabs_per_block = LANES // (2 * SSM_GROUP_CH)
    for j in range(n_slabs):
        m = j // slabs_per_block
        buf[j] = jnp.dot(ub[:, m * LANES:(m + 1) * LANES], bm_ref[j],
                         preferred_element_type=F32)

    for first in range(0, n_slabs, SCAN_GROUP):
        slabs = list(range(first, first + SCAN_GROUP))
        coefs = [(are_ref[j], aim_ref[j]) for j in slabs]
        init = tuple((st[j, :, :LANES], st[j, :, LANES:]) for j in slabs)

        def body(t, carry, slabs=slabs, coefs=coefs):
            row = t * SUBLANES
            new = []
            for (re, im), (are, aim), j in zip(carry, coefs, slabs):
                bre = buf[j, pl.ds(row, SUBLANES), :LANES]
                bim = buf[j, pl.ds(row, SUBLANES), LANES:]
                nre = are * re - aim * im + bre
                nim = are * im + aim * re + bim
                buf[j, pl.ds(row, SUBLANES), :LANES] = nre
                buf[j, pl.ds(row, SUBLANES), LANES:] = nim
                new.append((nre, nim))
            return tuple(new)

        fin = init
        for t in range(steps):
            fin = body(t, fin)
        for (re, im), j in zip(fin, slabs):
            st[j, :, :LANES] = re
            st[j, :, LANES:] = im

    ys = []
    for m in range(n_slabs // slabs_per_block):
        acc = None
        for j in range(m * slabs_per_block, (m + 1) * slabs_per_block):
            part = jnp.dot(buf[j].astype(BF16), cm_ref[j], preferred_element_type=F32)
            acc = part if acc is None else acc + part
        ys.append(acc)
    y = jnp.concatenate(ys, axis=1) + dsk_ref[...] * u
    z = jax.nn.gelu(y, approximate=True)
    gate = jax.nn.sigmoid(
        jnp.dot(z.astype(BF16), wg_ref[...], preferred_element_type=F32) + bg_ref[...])
    out = (z * gate).reshape(steps, bsz, width)
    o_ref[...] = jnp.swapaxes(out, 0, 1).astype(BF16)


def _ssm_branch(u, bmat, a_re, a_im, cmat, d_skip, w_glu, b_glu):
    bsz, s, width = u.shape
    steps = min(SCAN_STEPS, s)
    rows = steps * bsz
    n_slabs = bmat.shape[0]
    kern = functools.partial(_ssm_kernel, steps=steps, n_slabs=n_slabs)
    full = lambda a: pl.BlockSpec(a.shape, lambda i: (0,) * a.ndim)
    d_skip = d_skip.reshape(1, width)
    b_glu = b_glu.reshape(1, width)
    return pl.pallas_call(
        kern,
        grid=(s // steps,),
        in_specs=[pl.BlockSpec((bsz, steps, width), lambda i: (0, i, 0)),
                  full(bmat), full(a_re), full(a_im), full(cmat),
                  full(d_skip), full(w_glu), full(b_glu)],
        out_specs=pl.BlockSpec((bsz, steps, width), lambda i: (0, i, 0)),
        out_shape=jax.ShapeDtypeStruct((bsz, s, width), BF16),
        scratch_shapes=[pltpu.VMEM((n_slabs, rows, SLAB), F32),
                        pltpu.VMEM((n_slabs, bsz, SLAB), F32)],
        compiler_params=_params("arbitrary"),
        name="s5_branch",
    )(u, bmat, a_re, a_im, cmat, d_skip, w_glu, b_glu)


def _s5_matrices(lam_re, lam_im, log_dt, b_re, b_im, c_re, c_im, bsz):
    g, p = lam_re.shape
    h = b_re.shape[-1]
    lr = jnp.minimum(lam_re, -EIG_CLIP)
    li = lam_im
    dt = jnp.exp(log_dt)[:, None]
    mag = jnp.exp(lr * dt)
    a_re = mag * jnp.cos(li * dt)
    a_im = mag * jnp.sin(li * dt)
    den = lr * lr + li * li
    f_re = ((a_re - 1.0) * lr + a_im * li) / den
    f_im = (a_im * lr - (a_re - 1.0) * li) / den
    bb_re = f_re[..., None] * b_re - f_im[..., None] * b_im
    bb_im = f_re[..., None] * b_im + f_im[..., None] * b_re
    n_slabs = g // 2
    per_block = LANES // (2 * h)
    n_blocks = n_slabs // per_block
    eye_s = jnp.eye(per_block, dtype=F32)
    eye_g = jnp.eye(2, dtype=F32)

    def expand(w_hp, spec):
        w = w_hp.reshape(n_blocks, per_block, 2, h, p)
        return jnp.einsum(spec, w, eye_s, eye_g)

    b_spec = "mjghp,ja,gk->mjaghkp"
    bmat = jnp.concatenate(
        [expand(bb.transpose(0, 2, 1), b_spec).reshape(n_slabs, LANES, LANES)
         for bb in (bb_re, bb_im)], axis=2)
    c_spec = "mjghp,ja,gk->mjkpagh"
    cmat = jnp.concatenate(
        [expand(cc, c_spec).reshape(n_slabs, LANES, LANES) for cc in (c_re, -c_im)], axis=1)
    a_re_s = a_re.reshape(n_slabs, 1, 2 * p)
    a_im_s = a_im.reshape(n_slabs, 1, 2 * p)
    a_re_s = jnp.broadcast_to(a_re_s, (n_slabs, bsz, 2 * p))
    a_im_s = jnp.broadcast_to(a_im_s, (n_slabs, bsz, 2 * p))
    return bmat.astype(BF16), a_re_s, a_im_s, cmat.astype(BF16)


def _attn_kernel(ti_ref, tj_ref, q_ref, k_ref, vt_ref, ccol_ref, o_ref,
                 s_buf, p_buf, a_buf, m_sc, l_sc, acc_sc, *, tq, tk, n_diag, kinds):
    m_sc[...] = jnp.full_like(m_sc, -jnp.inf)
    l_sc[...] = jnp.zeros_like(l_sc)
    acc_sc[...] = jnp.zeros_like(acc_sc)
    p_buf[1] = jnp.zeros(p_buf.shape[1:], BF16)
    a_buf[1] = jnp.ones(a_buf.shape[1:], F32)

    n_steps = len(kinds)

    def kind_of(n):
        return kinds[min(max(n, 0), n_steps - 1)]

    def tile_of(n):
        i = ti_ref[n]
        return i, pl.multiple_of(i * tq, tq), pl.multiple_of(tj_ref[n] * tk, tk)

    def scores(n, slot, kind):
        masked, width = kind
        _, qs, ks = tile_of(n)
        kt = k_ref[0, 0, pl.ds(ks, width), :]
        for hh in range(2):
            ck = ccol_ref[0, 0, pl.ds(ks, width), hh:hh + 1]
            s = lax.dot_general(kt, q_ref[0, 0, hh, pl.ds(qs, tq), :], (((1,), (1,)), ((), ())),
                                preferred_element_type=F32) - ck
            if masked:
                kpos = ks + lax.broadcasted_iota(jnp.int32, s.shape, 0)
                qpos = qs + lax.broadcasted_iota(jnp.int32, s.shape, 1)
                s = jnp.where(kpos <= qpos, s, -jnp.inf)
            s_buf[slot, hh, :width, :] = s

    def softmax(n, slot, kind):
        _, width = kind
        i, _, _ = tile_of(n)
        for hh in range(2):
            m_old = m_sc[i, hh]
            m_new = jnp.maximum(m_old, jnp.max(s_buf[slot, hh, :width, :], axis=0, keepdims=True))
            m_sc[i, hh] = m_new
            alpha = jnp.exp2(m_old - m_new)
            pexp = jnp.exp2(s_buf[slot, hh, :width, :] - m_new)
            a_buf[slot, hh] = alpha
            p_buf[slot, hh, :width, :] = pexp.astype(BF16)

    def values(n, slot, kind):
        _, width = kind
        i, _, ks = tile_of(n)
        vt = vt_ref[0, 0, :, pl.ds(ks, width)]
        ones = jnp.ones((2 * SUBLANES, width), BF16)
        for hh in range(2):
            rows = slice(hh * HEAD_DIM, (hh + 1) * HEAD_DIM)
            lhs = jnp.concatenate([vt[rows, :], ones], axis=0)
            res = jnp.dot(lhs, p_buf[slot, hh, :width, :], preferred_element_type=F32)
            alpha = a_buf[slot, hh]
            acc_sc[i, rows, :] = alpha * acc_sc[i, rows, :] + res[:HEAD_DIM, :]
            l_sc[i, hh] = alpha * l_sc[i, hh] + res[HEAD_DIM:HEAD_DIM + 1, :]

    def step_pair(window):
        k_prev, k_even, k_odd, k_next = window

        def body(t, carry):
            n = 2 * t
            prev = max(n - 1, 0) if isinstance(n, int) else jnp.maximum(n - 1, 0)
            scores(n + 1, 1, k_odd)
            softmax(n, 0, k_even)
            values(prev, 1, k_prev)
            scores(n + 2, 0, k_next)
            softmax(n + 1, 1, k_odd)
            values(n, 0, k_even)
            return carry
        return body

    windows = [tuple(kind_of(2 * t + d) for d in (-1, 0, 1, 2)) for t in range(n_steps // 2)]
    scores(0, 0, kind_of(0))
    t0 = 0
    while t0 < len(windows):
        t1 = t0
        while t1 < len(windows) and windows[t1] == windows[t0]:
            t1 += 1
        if t1 - t0 > 1:
            lax.fori_loop(t0, t1, step_pair(windows[t0]), 0)
        else:
            step_pair(windows[t0])(t0, 0)
        t0 = t1
    values(n_steps - 1, 1, kind_of(n_steps - 1))

    def finalize(i, carry):
        out_t = jnp.concatenate([acc_sc[i, :HEAD_DIM, :] / l_sc[i, 0],
                                 acc_sc[i, HEAD_DIM:, :] / l_sc[i, 1]], axis=0)
        o_ref[0, pl.ds(pl.multiple_of(i * tq, tq), tq), :] = out_t.T.astype(BF16)
        return carry

    lax.fori_loop(0, n_diag, finalize, 0)


def _attention_steps(s, tile, wide):
    nq = s // tile
    per_wide = wide // tile
    diag = [(i, i) for i in range(nq)]
    narrow = [(i, j) for i in range(nq) for j in range(i // per_wide * per_wide, i)]
    broad = [(i, w * per_wide) for i in range(nq) for w in range(i // per_wide)]
    groups = [(diag, (True, tile)), (narrow, (False, tile)), (broad, (False, wide))]
    assert all(len(g) % 2 == 0 for g, _ in groups), "the pipeline advances two steps at a time"
    steps = [st for g, _ in groups for st in g]
    kinds = tuple(kind for g, kind in groups for _ in g)
    padded = steps + [steps[-1]] * 2
    ti = jnp.asarray([i for i, _ in padded], jnp.int32)
    tj = jnp.asarray([j for _, j in padded], jnp.int32)
    return ti, tj, nq, kinds


def _attention(q, k, v_t, cum_col):
    bsz, pairs, _, s, _ = q.shape
    tq = min(ATTN_Q_TILE, s)
    wide = min(ATTN_K_WIDE, s)
    ti, tj, nq, kinds = _attention_steps(s, tq, wide)
    tk = max(width for _, width in kinds)
    kern = functools.partial(_attn_kernel, tq=tq, tk=tq, n_diag=nq, kinds=kinds)
    grid_spec = pltpu.PrefetchScalarGridSpec(
        num_scalar_prefetch=2,
        grid=(bsz, pairs),
        in_specs=[pl.BlockSpec((1, 1, 2, s, LANES), lambda b, p, ti, tj: (b, p, 0, 0, 0)),
                  pl.BlockSpec((1, 1, s, LANES), lambda b, p, ti, tj: (b, p, 0, 0)),
                  pl.BlockSpec((1, 1, LANES, s), lambda b, p, ti, tj: (b, p, 0, 0)),
                  pl.BlockSpec((1, 1, s, 2), lambda b, p, ti, tj: (b, p, 0, 0))],
        out_specs=pl.BlockSpec((1, s, LANES), lambda b, p, ti, tj: (b, 0, p)),
        scratch_shapes=[pltpu.VMEM((2, 2, tk, tq), F32),
                        pltpu.VMEM((2, 2, tk, tq), BF16),
                        pltpu.VMEM((2, 2, 1, tq), F32),
                        pltpu.VMEM((nq, 2, 1, tq), F32),
                        pltpu.VMEM((nq, 2, 1, tq), F32),
                        pltpu.VMEM((nq, LANES, tq), F32)])
    return pl.pallas_call(
        kern,
        grid_spec=grid_spec,
        out_shape=jax.ShapeDtypeStruct((bsz, s, pairs * LANES), BF16),
        compiler_params=_params("arbitrary", "arbitrary"),
        name="forgetting_attention",
    )(ti, tj, q, k, v_t, cum_col)


def _merge_kernel(x_ref, mod_ref, gpre_ref, gpost_ref, wgate_ref, ys_ref, ya_ref,
                  wpa_ref, wpb_ref, wo_ref, o_ref, *, sub):
    b = pl.program_id(0)
    d = x_ref.shape[-1]
    for r0 in range(0, x_ref.shape[1], sub):
        rows = slice(r0, r0 + sub)
        x = x_ref[0, rows, :]
        h = _rmsnorm(x, gpre_ref[...]) * (1.0 + _mod_row(mod_ref, 1, b)) + _mod_row(mod_ref, 0, b)
        gates = jnp.dot(h.astype(BF16), wgate_ref[...], preferred_element_type=F32)
        pa = jnp.dot(ys_ref[0, rows, :], wpa_ref[...], preferred_element_type=F32)
        pb = jnp.dot(ya_ref[0, rows, :], wpb_ref[...], preferred_element_type=F32)
        merged = jax.nn.sigmoid(gates[:, :d]) * pa + jax.nn.sigmoid(gates[:, d:]) * pb
        y = jnp.dot(merged.astype(BF16), wo_ref[...], preferred_element_type=F32)
        o_ref[0, rows, :] = x + _mod_row(mod_ref, 2, b) * _rmsnorm(y, gpost_ref[...])


def _resident(a, layer=None):
    if layer is None:
        return pl.BlockSpec(a.shape, lambda b, i: (0,) * a.ndim, pipeline_mode=pl.Buffered(1))
    return pl.BlockSpec((None,) + a.shape[1:], lambda b, i: (layer,) + (0,) * (a.ndim - 1),
                        pipeline_mode=pl.Buffered(1))


def _merge(x, mod, g_pre, g_post, w_gate, y_ssm, y_att, w_pa, w_pb, w_o, layer):
    bsz, s, d = x.shape
    tm = min(2 * TOKEN_TILE, s)
    width = y_att.shape[-1]
    full = lambda a: pl.BlockSpec(a.shape, lambda b, i: (0,) * a.ndim)
    g_pre = g_pre.reshape(1, d)
    g_post = g_post.reshape(1, d)
    return pl.pallas_call(
        functools.partial(_merge_kernel, sub=min(TOKEN_TILE, tm)),
        grid=(bsz, s // tm),
        in_specs=[pl.BlockSpec((1, tm, d), lambda b, i: (b, i, 0)),
                  full(mod), full(g_pre), full(g_post), _resident(w_gate),
                  pl.BlockSpec((1, tm, width), lambda b, i: (b, i, 0)),
                  pl.BlockSpec((1, tm, width), lambda b, i: (b, i, 0)),
                  _resident(w_pa, layer), _resident(w_pb, layer), _resident(w_o, layer)],
        out_specs=pl.BlockSpec((1, tm, d), lambda b, i: (b, i, 0)),
        out_shape=jax.ShapeDtypeStruct(x.shape, F32),
        compiler_params=_params("arbitrary", "arbitrary"),
        name="gated_merge",
    )(x, mod, g_pre, g_post, w_gate, y_ssm, y_att, w_pa, w_pb, w_o)


def _ffn_chunks(d_ff):
    mxu = 2 * LANES
    n_tiles = d_ff // mxu
    first = (n_tiles + 1) // 2 * mxu
    return ((0, first), (first, d_ff)) if first < d_ff else ((0, d_ff),)


def _ffn_kernel(x_ref, mod_ref, gpre_ref, gpost_ref, wg_ref, wu_ref, wd_ref, o_ref, *, chunks, sub):
    b = pl.program_id(0)
    for r0 in range(0, x_ref.shape[1], sub):
        x = x_ref[0, r0:r0 + sub, :]
        h = _rmsnorm(x, gpre_ref[...]) * (1.0 + _mod_row(mod_ref, 4, b)) + _mod_row(mod_ref, 3, b)
        hb = h.astype(BF16)
        acc = None
        for c0, c1 in chunks:
            g = jnp.dot(hb, wg_ref[:, c0:c1], preferred_element_type=F32)
            up = jnp.dot(hb, wu_ref[:, c0:c1], preferred_element_type=F32)
            a = (g * jax.nn.sigmoid(g) * up).astype(BF16)
            part = jnp.dot(a, wd_ref[c0:c1, :], preferred_element_type=F32)
            acc = part if acc is None else acc + part
        o_ref[0, r0:r0 + sub, :] = x + _mod_row(mod_ref, 5, b) * _rmsnorm(acc, gpost_ref[...])


def _ffn(x, mod, g_pre, g_post, w_gate, w_up, w_down, layer):
    bsz, s, d = x.shape
    tm = min(2 * TOKEN_TILE, s)
    kern = functools.partial(_ffn_kernel, chunks=_ffn_chunks(w_gate.shape[-1]),
                             sub=min(TOKEN_TILE, tm))
    full = lambda a: pl.BlockSpec(a.shape, lambda b, i: (0,) * a.ndim)
    g_pre = g_pre.reshape(1, d)
    g_post = g_post.reshape(1, d)
    return pl.pallas_call(
        kern,
        grid=(bsz, s // tm),
        in_specs=[pl.BlockSpec((1, tm, d), lambda b, i: (b, i, 0)),
                  full(mod), full(g_pre), full(g_post),
                  _resident(w_gate, layer), _resident(w_up, layer), _resident(w_down, layer)],
        out_specs=pl.BlockSpec((1, tm, d), lambda b, i: (b, i, 0)),
        out_shape=jax.ShapeDtypeStruct(x.shape, F32),
        compiler_params=_params("arbitrary", "arbitrary"),
        name="swiglu_ffn",
    )(x, mod, g_pre, g_post, w_gate, w_up, w_down)


def kernel(x, c, w_ada, b_ada, g_pre_mix, g_post_mix, g_pre_ffn, g_post_ffn, w_in, lam_re, lam_im, log_dt, b_re, b_im, c_re, c_im, d_skip, w_glu, b_glu, b_f, w_pa, w_pb, w_o, w_ffn_gate, w_ffn_up, w_ffn_down):
    bsz, s, d = x.shape
    depth = w_in.shape[0]
    ssm_w = d_skip.shape[1]
    heads = b_f.shape[1]
    attn_w = heads * HEAD_DIM
    pairs = attn_w // LANES
    main_w = ssm_w + 3 * attn_w
    assert bsz == SUBLANES, "the S5 scan keeps the batch on the sublane axis"

    mod_all = _modulation(c, w_ada, b_ada)
    w_in_b = w_in.astype(BF16)
    w_pa_b, w_pb_b, w_o_b = w_pa.astype(BF16), w_pb.astype(BF16), w_o.astype(BF16)
    w_ffn = (w_ffn_gate.astype(BF16), w_ffn_up.astype(BF16), w_ffn_down.astype(BF16))
    for l in range(depth):
        mod = mod_all[l]
        uqk_w = ssm_w + 2 * attn_w
        w_main = w_in_b[l, :, :uqk_w]
        w_vf_t = w_in_b[l, :, uqk_w:main_w + heads].T
        w_gate = w_in_b[l, :, main_w + heads:]

        u, q, k, v_t, f_t = _in_projection(x, mod, g_pre_mix[l], w_main, w_vf_t, ssm_w, attn_w)

        cum_col = _forget_cumsum(f_t, b_f[l])
        y_att = _attention(q, k, v_t, cum_col)

        bmat, a_re, a_im, cmat = _s5_matrices(lam_re[l], lam_im[l], log_dt[l], b_re[l], b_im[l],
                                              c_re[l], c_im[l], bsz)
        y_ssm = _ssm_branch(u, bmat, a_re, a_im, cmat,
                            d_skip[l], w_glu[l].astype(BF16), b_glu[l])

        x = _merge(x, mod, g_pre_mix[l], g_post_mix[l], w_gate, y_ssm, y_att,
                   w_pa_b, w_pb_b, w_o_b, l)
        x = _ffn(x, mod, g_pre_ffn[l], g_post_ffn[l], *w_ffn, l)
    return x
```

```python
import functools
import math

import jax
import jax.numpy as jnp
from jax import lax
from jax.experimental import pallas as pl
from jax.experimental.pallas import tpu as pltpu

F32 = jnp.float32
BF16 = jnp.bfloat16

SSM_GROUP_CH = 16
SSM_STATE = 64
HEAD_DIM = 64
N_MOD = 6
RMS_EPS = 1e-6
EIG_CLIP = 1e-4
LOG2E = math.log2(math.e)

LANES = 128
SUBLANES = 8
SLAB = 2 * LANES
VMEM_LIMIT = 56 * 1024 * 1024

TOKEN_TILE = 512
SCAN_STEPS = 64
ATTN_Q_TILE = 512
ATTN_K_WIDE = 1024
SCAN_GROUP = 4
SCAN_UNROLL = 4


def _params(*sem):
    return pltpu.CompilerParams(dimension_semantics=sem, vmem_limit_bytes=VMEM_LIMIT)


def _rmsnorm(x, g):
    r = lax.rsqrt(jnp.mean(x * x, axis=-1, keepdims=True) + RMS_EPS)
    return x * r * g


def _mod_row(mod_ref, j, b):
    return mod_ref[j, pl.ds(b, 1), :]


def _mod_kernel(c_ref, w_ref, b_ref, o_ref):
    c = c_ref[...]
    cond = c * jax.nn.sigmoid(c)
    o_ref[0, 0] = jnp.dot(cond.astype(BF16), w_ref[0].astype(BF16),
                          preferred_element_type=F32) + b_ref[0, 0]


def _modulation(c, w_ada, b_ada):
    depth, d, nd = w_ada.shape
    bsz = c.shape[0]
    n = nd // d
    return pl.pallas_call(
        _mod_kernel,
        grid=(depth, n),
        in_specs=[pl.BlockSpec((bsz, d), lambda l, j: (0, 0)),
                  pl.BlockSpec((1, d, d), lambda l, j: (l, 0, j)),
                  pl.BlockSpec((1, 1, 1, d), lambda l, j: (l, j, 0, 0))],
        out_specs=pl.BlockSpec((1, 1, bsz, d), lambda l, j: (l, j, 0, 0)),
        out_shape=jax.ShapeDtypeStruct((depth, n, bsz, d), F32),
        compiler_params=_params("arbitrary", "arbitrary"),
        name="adaln_mod",
    )(c, w_ada, b_ada.reshape(depth, n, 1, d))


def _in_kernel(x_ref, mod_ref, g_ref, w_ref, wvf_ref, u_ref, q_ref, k_ref, vt_ref, f_ref,
               *, ssm_w, attn_w):
    b = pl.program_id(0)
    x = x_ref[0]
    h = _rmsnorm(x, g_ref[...]) * (1.0 + _mod_row(mod_ref, 1, b)) + _mod_row(mod_ref, 0, b)
    hb = h.astype(BF16)
    proj = jnp.dot(hb, w_ref[...], preferred_element_type=F32)
    u_ref[0] = proj[:, :ssm_w]
    scale = HEAD_DIM ** -0.5 * LOG2E
    proj_t = lax.dot_general(wvf_ref[...], hb, (((1,), (1,)), ((), ())),
                             preferred_element_type=F32)
    lane = lax.broadcasted_iota(jnp.int32, (x.shape[0], LANES), 1)
    for p in range(attn_w // LANES):
        c0 = ssm_w + p * LANES
        q_pair = proj[:, c0:c0 + LANES] * scale
        q_ref[0, p, 0] = jnp.where(lane < HEAD_DIM, q_pair, 0.0).astype(BF16)
        q_ref[0, p, 1] = jnp.where(lane >= HEAD_DIM, q_pair, 0.0).astype(BF16)
        k_ref[0, p] = proj[:, c0 + attn_w:c0 + attn_w + LANES].astype(BF16)
        vt_ref[0, p] = proj_t[p * LANES:(p + 1) * LANES, :].astype(BF16)
    f_ref[0] = proj_t[attn_w:, :]


def _in_projection(x, mod, g, w_main, w_vf_t, ssm_w, attn_w):
    bsz, s, d = x.shape
    tm = min(TOKEN_TILE, s)
    pairs = attn_w // LANES
    heads = w_vf_t.shape[0] - attn_w
    kern = functools.partial(_in_kernel, ssm_w=ssm_w, attn_w=attn_w)
    qk_shape = jax.ShapeDtypeStruct((bsz, pairs, s, LANES), BF16)
    qk_spec = pl.BlockSpec((1, pairs, tm, LANES), lambda b, i: (b, 0, i, 0))
    return pl.pallas_call(
        kern,
        grid=(bsz, s // tm),
        in_specs=[pl.BlockSpec((1, tm, d), lambda b, i: (b, i, 0)),
                  pl.BlockSpec(mod.shape, lambda b, i: (0, 0, 0)),
                  pl.BlockSpec((1, d), lambda b, i: (0, 0)),
                  pl.BlockSpec(w_main.shape, lambda b, i: (0, 0)),
                  pl.BlockSpec(w_vf_t.shape, lambda b, i: (0, 0))],
        out_specs=[pl.BlockSpec((1, tm, ssm_w), lambda b, i: (b, i, 0)),
                   pl.BlockSpec((1, pairs, 2, tm, LANES), lambda b, i: (b, 0, 0, i, 0)),
                   qk_spec,
                   pl.BlockSpec((1, pairs, LANES, tm), lambda b, i: (b, 0, 0, i)),
                   pl.BlockSpec((1, heads, tm), lambda b, i: (b, 0, i))],
        out_shape=[jax.ShapeDtypeStruct((bsz, s, ssm_w), F32),
                   jax.ShapeDtypeStruct((bsz, pairs, 2, s, LANES), BF16),
                   qk_shape,
                   jax.ShapeDtypeStruct((bsz, pairs, LANES, s), BF16),
                   jax.ShapeDtypeStruct((bsz, heads, s), F32)],
        compiler_params=_params("arbitrary", "arbitrary"),
        name="in_projection",
    )(x, mod, g.reshape(1, d), w_main, w_vf_t)


def _cum_kernel(f_ref, bf_ref, o_ref):
    x = jax.nn.log_sigmoid(f_ref[0] + bf_ref[...])
    s = x.shape[1]
    lane = lax.broadcasted_iota(jnp.int32, x.shape, 1)
    shift = 1
    while shift < s:
        x = x + jnp.where(lane >= shift, pltpu.roll(x, shift, axis=1), 0.0)
        shift *= 2
    cum_t = (x * LOG2E).T
    for p in range(o_ref.shape[1]):
        o_ref[0, p] = cum_t[:, 2 * p:2 * p + 2]


def _forget_cumsum(f_t, b_f):
    bsz, heads, s = f_t.shape
    pairs = heads // 2
    return pl.pallas_call(
        _cum_kernel,
        grid=(bsz,),
        in_specs=[pl.BlockSpec((1, heads, s), lambda b: (b, 0, 0)),
                  pl.BlockSpec((heads, 1), lambda b: (0, 0))],
        out_specs=pl.BlockSpec((1, pairs, s, 2), lambda b: (b, 0, 0, 0)),
        out_shape=jax.ShapeDtypeStruct((bsz, pairs, s, 2), F32),
        compiler_params=_params("arbitrary"),
        name="forget_cumsum",
    )(f_t, b_f.reshape(heads, 1))


def _ssm_kernel(u_ref, bm_ref, are_ref, aim_ref, cm_ref, dsk_ref, wg_ref, bg_ref, o_ref,
                buf, st, *, steps, n_slabs):
    @pl.when(pl.program_id(0) == 0)
    def _():
        st[...] = jnp.zeros_like(st)

    bsz, _, width = u_ref.shape
    rows = steps * bsz
    u = jnp.swapaxes(u_ref[...], 0, 1).reshape(rows, width)
    ub = u.astype(BF16)
    slabs_per_block = LANES // (2 * SSM_GROUP_CH)
    for j in range(n_slabs):
        m = j // slabs_per_block
        buf[j] = jnp.dot(ub[:, m * LANES:(m + 1) * LANES], bm_ref[j],
                         preferred_element_type=F32)

    for first in range(0, n_slabs, SCAN_GROUP):
        slabs = list(range(first, first + SCAN_GROUP))
        coefs = [(are_ref[j], aim_ref[j]) for j in slabs]
        init = tuple((st[j, :, :LANES], st[j, :, LANES:]) for j in slabs)

        def body(t, carry, slabs=slabs, coefs=coefs):
            row = t * SUBLANES
            new = []
            for (re, im), (are, aim), j in zip(carry, coefs, slabs):
                bre = buf[j, pl.ds(row, SUBLANES), :LANES]
                bim = buf[j, pl.ds(row, SUBLANES), LANES:]
                nre = are * re - aim * im + bre
                nim = are * im + aim * re + bim
                buf[j, pl.ds(row, SUBLANES), :LANES] = nre
                buf[j, pl.ds(row, SUBLANES), LANES:] = nim
                new.append((nre, nim))
            return tuple(new)

        fin = init
        for t in range(steps):
            fin = body(t, fin)
        for (re, im), j in zip(fin, slabs):
            st[j, :, :LANES] = re
            st[j, :, LANES:] = im

    ys = []
    for m in range(n_slabs // slabs_per_block):
        acc = None
        for j in range(m * slabs_per_block, (m + 1) * slabs_per_block):
            part = jnp.dot(buf[j].astype(BF16), cm_ref[j], preferred_element_type=F32)
            acc = part if acc is None else acc + part
        ys.append(acc)
    y = jnp.concatenate(ys, axis=1) + dsk_ref[...] * u
    z = jax.nn.gelu(y, approximate=True)
    gate = jax.nn.sigmoid(
        jnp.dot(z.astype(BF16), wg_ref[...], preferred_element_type=F32) + bg_ref[...])
    out = (z * gate).reshape(steps, bsz, width)
    o_ref[...] = jnp.swapaxes(out, 0, 1).astype(BF16)


def _ssm_branch(u, bmat, a_re, a_im, cmat, d_skip, w_glu, b_glu):
    bsz, s, width = u.shape
    steps = min(SCAN_STEPS, s)
    rows = steps * bsz
    n_slabs = bmat.shape[0]
    kern = functools.partial(_ssm_kernel, steps=steps, n_slabs=n_slabs)
    full = lambda a: pl.BlockSpec(a.shape, lambda i: (0,) * a.ndim)
    d_skip = d_skip.reshape(1, width)
    b_glu = b_glu.reshape(1, width)
    return pl.pallas_call(
        kern,
        grid=(s // steps,),
        in_specs=[pl.BlockSpec((bsz, steps, width), lambda i: (0, i, 0)),
                  full(bmat), full(a_re), full(a_im), full(cmat),
                  full(d_skip), full(w_glu), full(b_glu)],
        out_specs=pl.BlockSpec((bsz, steps, width), lambda i: (0, i, 0)),
        out_shape=jax.ShapeDtypeStruct((bsz, s, width), BF16),
        scratch_shapes=[pltpu.VMEM((n_slabs, rows, SLAB), F32),
                        pltpu.VMEM((n_slabs, bsz, SLAB), F32)],
        compiler_params=_params("arbitrary"),
        name="s5_branch",
    )(u, bmat, a_re, a_im, cmat, d_skip, w_glu, b_glu)


def _s5_matrices(lam_re, lam_im, log_dt, b_re, b_im, c_re, c_im, bsz):
    g, p = lam_re.shape
    h = b_re.shape[-1]
    lr = jnp.minimum(lam_re, -EIG_CLIP)
    li = lam_im
    dt = jnp.exp(log_dt)[:, None]
    mag = jnp.exp(lr * dt)
    a_re = mag * jnp.cos(li * dt)
    a_im = mag * jnp.sin(li * dt)
    den = lr * lr + li * li
    f_re = ((a_re - 1.0) * lr + a_im * li) / den
    f_im = (a_im * lr - (a_re - 1.0) * li) / den
    bb_re = f_re[..., None] * b_re - f_im[..., None] * b_im
    bb_im = f_re[..., None] * b_im + f_im[..., None] * b_re
    n_slabs = g // 2
    per_block = LANES // (2 * h)
    n_blocks = n_slabs // per_block
    eye_s = jnp.eye(per_block, dtype=F32)
    eye_g = jnp.eye(2, dtype=F32)

    def expand(w_hp, spec):
        w = w_hp.reshape(n_blocks, per_block, 2, h, p)
        return jnp.einsum(spec, w, eye_s, eye_g)

    b_spec = "mjghp,ja,gk->mjaghkp"
    bmat = jnp.concatenate(
        [expand(bb.transpose(0, 2, 1), b_spec).reshape(n_slabs, LANES, LANES)
         for bb in (bb_re, bb_im)], axis=2)
    c_spec = "mjghp,ja,gk->mjkpagh"
    cmat = jnp.concatenate(
        [expand(cc, c_spec).reshape(n_slabs, LANES, LANES) for cc in (c_re, -c_im)], axis=1)
    a_re_s = a_re.reshape(n_slabs, 1, 2 * p)
    a_im_s = a_im.reshape(n_slabs, 1, 2 * p)
    a_re_s = jnp.broadcast_to(a_re_s, (n_slabs, bsz, 2 * p))
    a_im_s = jnp.broadcast_to(a_im_s, (n_slabs, bsz, 2 * p))
    return bmat.astype(BF16), a_re_s, a_im_s, cmat.astype(BF16)


def _attn_kernel(ti_ref, tj_ref, q_ref, k_ref, vt_ref, ccol_ref, o_ref,
                 s_buf, mx_buf, p_buf, a_buf, m_sc, l_sc, acc_sc, *, tq, tk, n_diag, kinds):
    m_sc[...] = jnp.full_like(m_sc, -jnp.inf)
    l_sc[...] = jnp.zeros_like(l_sc)
    acc_sc[...] = jnp.zeros_like(acc_sc)
    p_buf[1] = jnp.zeros(p_buf.shape[1:], BF16)
    a_buf[1] = jnp.ones(a_buf.shape[1:], F32)

    n_steps = len(kinds)

    def kind_of(n):
        return kinds[min(max(n, 0), n_steps - 1)]

    def tile_of(n):
        i = ti_ref[n]
        return i, pl.multiple_of(i * tq, tq), pl.multiple_of(tj_ref[n] * tk, tk)

    def scores(n, slot, kind):
        masked, width = kind
        _, qs, ks = tile_of(n)
        kt = k_ref[0, 0, pl.ds(ks, width), :]
        for hh in range(2):
            ck = ccol_ref[0, 0, pl.ds(ks, width), hh:hh + 1]
            s = lax.dot_general(kt, q_ref[0, 0, hh, pl.ds(qs, tq), :], (((1,), (1,)), ((), ())),
                                preferred_element_type=F32) - ck
            if masked:
                kpos = ks + lax.broadcasted_iota(jnp.int32, s.shape, 0)
                qpos = qs + lax.broadcasted_iota(jnp.int32, s.shape, 1)
                s = jnp.where(kpos <= qpos, s, -jnp.inf)
            s_buf[slot, hh, :width, :] = s
            mx_buf[slot, hh] = jnp.max(s, axis=0, keepdims=True)

    def softmax(n, slot, kind):
        _, width = kind
        i, _, _ = tile_of(n)
        for hh in range(2):
            m_old = m_sc[i, hh]
            m_new = jnp.maximum(m_old, mx_buf[slot, hh])
            m_sc[i, hh] = m_new
            alpha = jnp.exp2(m_old - m_new)
            pexp = jnp.exp2(s_buf[slot, hh, :width, :] - m_new)
            a_buf[slot, hh] = alpha
            p_buf[slot, hh, :width, :] = pexp.astype(BF16)

    def values(n, slot, kind):
        _, width = kind
        i, _, ks = tile_of(n)
        vt = vt_ref[0, 0, :, pl.ds(ks, width)]
        ones = jnp.ones((2 * SUBLANES, width), BF16)
        for hh in range(2):
            rows = slice(hh * HEAD_DIM, (hh + 1) * HEAD_DIM)
            lhs = jnp.concatenate([vt[rows, :], ones], axis=0)
            res = jnp.dot(lhs, p_buf[slot, hh, :width, :], preferred_element_type=F32)
            alpha = a_buf[slot, hh]
            acc_sc[i, rows, :] = alpha * acc_sc[i, rows, :] + res[:HEAD_DIM, :]
            l_sc[i, hh] = alpha * l_sc[i, hh] + res[HEAD_DIM:HEAD_DIM + 1, :]

    def step_pair(window):
        k_prev, k_even, k_odd, k_next = window

        def body(t, carry):
            n = 2 * t
            prev = max(n - 1, 0) if isinstance(n, int) else jnp.maximum(n - 1, 0)
            scores(n + 1, 1, k_odd)
            softmax(n, 0, k_even)
            values(prev, 1, k_prev)
            scores(n + 2, 0, k_next)
            softmax(n + 1, 1, k_odd)
            values(n, 0, k_even)
            return carry
        return body

    windows = [tuple(kind_of(2 * t + d) for d in (-1, 0, 1, 2)) for t in range(n_steps // 2)]
    scores(0, 0, kind_of(0))
    t0 = 0
    while t0 < len(windows):
        t1 = t0
        while t1 < len(windows) and windows[t1] == windows[t0]:
            t1 += 1
        if t1 - t0 > 1:
            lax.fori_loop(t0, t1, step_pair(windows[t0]), 0)
        else:
            step_pair(windows[t0])(t0, 0)
        t0 = t1
    values(n_steps - 1, 1, kind_of(n_steps - 1))

    def finalize(i, carry):
        out_t = jnp.concatenate([acc_sc[i, :HEAD_DIM, :] / l_sc[i, 0],
                                 acc_sc[i, HEAD_DIM:, :] / l_sc[i, 1]], axis=0)
        o_ref[0, pl.ds(pl.multiple_of(i * tq, tq), tq), :] = out_t.T.astype(BF16)
        return carry

    lax.fori_loop(0, n_diag, finalize, 0)


def _attention_steps(s, tile, wide):
    nq = s // tile
    per_wide = wide // tile
    diag = [(i, i) for i in range(nq)]
    narrow = [(i, j) for i in range(nq) for j in range(i // per_wide * per_wide, i)]
    broad = [(i, w * per_wide) for i in range(nq) for w in range(i // per_wide)]
    groups = [(diag, (True, tile)), (narrow, (False, tile)), (broad, (False, wide))]
    assert all(len(g) % 2 == 0 for g, _ in groups), "the pipeline advances two steps at a time"
    steps = [st for g, _ in groups for st in g]
    kinds = tuple(kind for g, kind in groups for _ in g)
    padded = steps + [steps[-1]] * 2
    ti = jnp.asarray([i for i, _ in padded], jnp.int32)
    tj = jnp.asarray([j for _, j in padded], jnp.int32)
    return ti, tj, nq, kinds


def _attention(q, k, v_t, cum_col):
    bsz, pairs, _, s, _ = q.shape
    tq = min(ATTN_Q_TILE, s)
    wide = min(ATTN_K_WIDE, s)
    ti, tj, nq, kinds = _attention_steps(s, tq, wide)
    tk = max(width for _, width in kinds)
    kern = functools.partial(_attn_kernel, tq=tq, tk=tq, n_diag=nq, kinds=kinds)
    grid_spec = pltpu.PrefetchScalarGridSpec(
        num_scalar_prefetch=2,
        grid=(bsz, pairs),
        in_specs=[pl.BlockSpec((1, 1, 2, s, LANES), lambda b, p, ti, tj: (b, p, 0, 0, 0)),
                  pl.BlockSpec((1, 1, s, LANES), lambda b, p, ti, tj: (b, p, 0, 0)),
                  pl.BlockSpec((1, 1, LANES, s), lambda b, p, ti, tj: (b, p, 0, 0)),
                  pl.BlockSpec((1, 1, s, 2), lambda b, p, ti, tj: (b, p, 0, 0))],
        out_specs=pl.BlockSpec((1, s, LANES), lambda b, p, ti, tj: (b, 0, p)),
        scratch_shapes=[pltpu.VMEM((2, 2, tk, tq), F32),
                        pltpu.VMEM((2, 2, 1, tq), F32),
                        pltpu.VMEM((2, 2, tk, tq), BF16),
                        pltpu.VMEM((2, 2, 1, tq), F32),
                        pltpu.VMEM((nq, 2, 1, tq), F32),
                        pltpu.VMEM((nq, 2, 1, tq), F32),
                        pltpu.VMEM((nq, LANES, tq), F32)])
    return pl.pallas_call(
        kern,
        grid_spec=grid_spec,
        out_shape=jax.ShapeDtypeStruct((bsz, s, pairs * LANES), BF16),
        compiler_params=_params("arbitrary", "arbitrary"),
        name="forgetting_attention",
    )(ti, tj, q, k, v_t, cum_col)


def _merge_kernel(x_ref, mod_ref, gpre_ref, gpost_ref, wgate_ref, ys_ref, ya_ref,
                  wpa_ref, wpb_ref, wo_ref, o_ref, *, sub):
    b = pl.program_id(0)
    d = x_ref.shape[-1]
    for r0 in range(0, x_ref.shape[1], sub):
        rows = slice(r0, r0 + sub)
        x = x_ref[0, rows, :]
        h = _rmsnorm(x, gpre_ref[...]) * (1.0 + _mod_row(mod_ref, 1, b)) + _mod_row(mod_ref, 0, b)
        gates = jnp.dot(h.astype(BF16), wgate_ref[...], preferred_element_type=F32)
        pa = jnp.dot(ys_ref[0, rows, :], wpa_ref[...], preferred_element_type=F32)
        pb = jnp.dot(ya_ref[0, rows, :], wpb_ref[...], preferred_element_type=F32)
        merged = jax.nn.sigmoid(gates[:, :d]) * pa + jax.nn.sigmoid(gates[:, d:]) * pb
        y = jnp.dot(merged.astype(BF16), wo_ref[...], preferred_element_type=F32)
        o_ref[0, rows, :] = x + _mod_row(mod_ref, 2, b) * _rmsnorm(y, gpost_ref[...])


def _resident(a, layer=None):
    if layer is None:
        return pl.BlockSpec(a.shape, lambda b, i: (0,) * a.ndim, pipeline_mode=pl.Buffered(1))
    return pl.BlockSpec((None,) + a.shape[1:], lambda b, i: (layer,) + (0,) * (a.ndim - 1),
                        pipeline_mode=pl.Buffered(1))


def _merge(x, mod, g_pre, g_post, w_gate, y_ssm, y_att, w_pa, w_pb, w_o, layer):
    bsz, s, d = x.shape
    tm = min(2 * TOKEN_TILE, s)
    width = y_att.shape[-1]
    full = lambda a: pl.BlockSpec(a.shape, lambda b, i: (0,) * a.ndim)
    g_pre = g_pre.reshape(1, d)
    g_post = g_post.reshape(1, d)
    return pl.pallas_call(
        functools.partial(_merge_kernel, sub=min(TOKEN_TILE, tm)),
        grid=(bsz, s // tm),
        in_specs=[pl.BlockSpec((1, tm, d), lambda b, i: (b, i, 0)),
                  full(mod), full(g_pre), full(g_post), _resident(w_gate),
                  pl.BlockSpec((1, tm, width), lambda b, i: (b, i, 0)),
                  pl.BlockSpec((1, tm, width), lambda b, i: (b, i, 0)),
                  _resident(w_pa, layer), _resident(w_pb, layer), _resident(w_o, layer)],
        out_specs=pl.BlockSpec((1, tm, d), lambda b, i: (b, i, 0)),
        out_shape=jax.ShapeDtypeStruct(x.shape, F32),
        compiler_params=_params("arbitrary", "arbitrary"),
        name="gated_merge",
    )(x, mod, g_pre, g_post, w_gate, y_ssm, y_att, w_pa, w_pb, w_o)


def _ffn_chunks(d_ff):
    mxu = 2 * LANES
    n_tiles = d_ff // mxu
    first = (n_tiles + 1) // 2 * mxu
    return ((0, first), (first, d_ff)) if first < d_ff else ((0, d_ff),)


def _ffn_kernel(x_ref, mod_ref, gpre_ref, gpost_ref, wg_ref, wu_ref, wd_ref, o_ref, *, chunks, sub):
    b = pl.program_id(0)
    for r0 in range(0, x_ref.shape[1], sub):
        x = x_ref[0, r0:r0 + sub, :]
        h = _rmsnorm(x, gpre_ref[...]) * (1.0 + _mod_row(mod_ref, 4, b)) + _mod_row(mod_ref, 3, b)
        hb = h.astype(BF16)
        acc = None
        for c0, c1 in chunks:
            g = jnp.dot(hb, wg_ref[:, c0:c1], preferred_element_type=F32)
            up = jnp.dot(hb, wu_ref[:, c0:c1], preferred_element_type=F32)
            a = (g * jax.nn.sigmoid(g) * up).astype(BF16)
            part = jnp.dot(a, wd_ref[c0:c1, :], preferred_element_type=F32)
            acc = part if acc is None else acc + part
        o_ref[0, r0:r0 + sub, :] = x + _mod_row(mod_ref, 5, b) * _rmsnorm(acc, gpost_ref[...])


def _ffn(x, mod, g_pre, g_post, w_gate, w_up, w_down, layer):
    bsz, s, d = x.shape
    tm = min(2 * TOKEN_TILE, s)
    kern = functools.partial(_ffn_kernel, chunks=_ffn_chunks(w_gate.shape[-1]),
                             sub=min(TOKEN_TILE, tm))
    full = lambda a: pl.BlockSpec(a.shape, lambda b, i: (0,) * a.ndim)
    g_pre = g_pre.reshape(1, d)
    g_post = g_post.reshape(1, d)
    return pl.pallas_call(
        kern,
        grid=(bsz, s // tm),
        in_specs=[pl.BlockSpec((1, tm, d), lambda b, i: (b, i, 0)),
                  full(mod), full(g_pre), full(g_post),
                  _resident(w_gate, layer), _resident(w_up, layer), _resident(w_down, layer)],
        out_specs=pl.BlockSpec((1, tm, d), lambda b, i: (b, i, 0)),
        out_shape=jax.ShapeDtypeStruct(x.shape, F32),
        compiler_params=_params("arbitrary", "arbitrary"),
        name="swiglu_ffn",
    )(x, mod, g_pre, g_post, w_gate, w_up, w_down)


def kernel(x, c, w_ada, b_ada, g_pre_mix, g_post_mix, g_pre_ffn, g_post_ffn, w_in, lam_re, lam_im, log_dt, b_re, b_im, c_re, c_im, d_skip, w_glu, b_glu, b_f, w_pa, w_pb, w_o, w_ffn_gate, w_ffn_up, w_ffn_down):
    bsz, s, d = x.shape
    depth = w_in.shape[0]
    ssm_w = d_skip.shape[1]
    heads = b_f.shape[1]
    attn_w = heads * HEAD_DIM
    pairs = attn_w // LANES
    main_w = ssm_w + 3 * attn_w
    assert bsz == SUBLANES, "the S5 scan keeps the batch on the sublane axis"

    mod_all = _modulation(c, w_ada, b_ada)
    w_in_b = w_in.astype(BF16)
    w_pa_b, w_pb_b, w_o_b = w_pa.astype(BF16), w_pb.astype(BF16), w_o.astype(BF16)
    w_ffn = (w_ffn_gate.astype(BF16), w_ffn_up.astype(BF16), w_ffn_down.astype(BF16))
    for l in range(depth):
        mod = mod_all[l]
        uqk_w = ssm_w + 2 * attn_w
        w_main = w_in_b[l, :, :uqk_w]
        w_vf_t = w_in_b[l, :, uqk_w:main_w + heads].T
        w_gate = w_in_b[l, :, main_w + heads:]

        u, q, k, v_t, f_t = _in_projection(x, mod, g_pre_mix[l], w_main, w_vf_t, ssm_w, attn_w)

        cum_col = _forget_cumsum(f_t, b_f[l])
        y_att = _attention(q, k, v_t, cum_col)

        bmat, a_re, a_im, cmat = _s5_matrices(lam_re[l], lam_im[l], log_dt[l], b_re[l], b_im[l],
                                              c_re[l], c_im[l], bsz)
        y_ssm = _ssm_branch(u, bmat, a_re, a_im, cmat,
                            d_skip[l], w_glu[l].astype(BF16), b_glu[l])

        x = _merge(x, mod, g_pre_mix[l], g_post_mix[l], w_gate, y_ssm, y_att,
                   w_pa_b, w_pb_b, w_o_b, l)
        x = _ffn(x, mod, g_pre_ffn[l], g_post_ffn[l], *w_ffn, l)
    return x
```

```python
import functools
import math

import jax
import jax.numpy as jnp
from jax import lax
from jax.experimental import pallas as pl
from jax.experimental.pallas import tpu as pltpu

F32 = jnp.float32
BF16 = jnp.bfloat16

SSM_GROUP_CH = 16
SSM_STATE = 64
HEAD_DIM = 64
N_MOD = 6
RMS_EPS = 1e-6
EIG_CLIP = 1e-4
LOG2E = math.log2(math.e)

LANES = 128
SUBLANES = 8
SLAB = 2 * LANES
VMEM_LIMIT = 56 * 1024 * 1024

TOKEN_TILE = 512
SCAN_STEPS = 64
ATTN_Q_TILE = 512
ATTN_K_WIDE = 1024
SCAN_GROUP = 4
SCAN_UNROLL = 4


def _params(*sem):
    return pltpu.CompilerParams(dimension_semantics=sem, vmem_limit_bytes=VMEM_LIMIT)


def _rmsnorm(x, g):
    r = lax.rsqrt(jnp.mean(x * x, axis=-1, keepdims=True) + RMS_EPS)
    return x * r * g


def _mod_row(mod_ref, j, b):
    return mod_ref[j, pl.ds(b, 1), :]


def _mod_kernel(c_ref, w_ref, b_ref, o_ref):
    c = c_ref[...]
    cond = c * jax.nn.sigmoid(c)
    o_ref[0, 0] = jnp.dot(cond.astype(BF16), w_ref[0].astype(BF16),
                          preferred_element_type=F32) + b_ref[0, 0]


def _modulation(c, w_ada, b_ada):
    depth, d, nd = w_ada.shape
    bsz = c.shape[0]
    n = nd // d
    return pl.pallas_call(
        _mod_kernel,
        grid=(depth, n),
        in_specs=[pl.BlockSpec((bsz, d), lambda l, j: (0, 0)),
                  pl.BlockSpec((1, d, d), lambda l, j: (l, 0, j)),
                  pl.BlockSpec((1, 1, 1, d), lambda l, j: (l, j, 0, 0))],
        out_specs=pl.BlockSpec((1, 1, bsz, d), lambda l, j: (l, j, 0, 0)),
        out_shape=jax.ShapeDtypeStruct((depth, n, bsz, d), F32),
        compiler_params=_params("arbitrary", "arbitrary"),
        name="adaln_mod",
    )(c, w_ada, b_ada.reshape(depth, n, 1, d))


def _in_kernel(x_ref, mod_ref, g_ref, w_ref, wvf_ref, u_ref, q_ref, k_ref, vt_ref, f_ref,
               *, ssm_w, attn_w):
    b = pl.program_id(0)
    x = x_ref[0]
    h = _rmsnorm(x, g_ref[...]) * (1.0 + _mod_row(mod_ref, 1, b)) + _mod_row(mod_ref, 0, b)
    hb = h.astype(BF16)
    proj = jnp.dot(hb, w_ref[...], preferred_element_type=F32)
    u_ref[0] = proj[:, :ssm_w]
    scale = HEAD_DIM ** -0.5 * LOG2E
    proj_t = lax.dot_general(wvf_ref[...], hb, (((1,), (1,)), ((), ())),
                             preferred_element_type=F32)
    lane = lax.broadcasted_iota(jnp.int32, (x.shape[0], LANES), 1)
    for p in range(attn_w // LANES):
        c0 = ssm_w + p * LANES
        q_pair = proj[:, c0:c0 + LANES] * scale
        q_ref[0, p, 0] = jnp.where(lane < HEAD_DIM, q_pair, 0.0).astype(BF16)
        q_ref[0, p, 1] = jnp.where(lane >= HEAD_DIM, q_pair, 0.0).astype(BF16)
        k_ref[0, p] = proj[:, c0 + attn_w:c0 + attn_w + LANES].astype(BF16)
        vt_ref[0, p] = proj_t[p * LANES:(p + 1) * LANES, :].astype(BF16)
    f_ref[0] = proj_t[attn_w:, :]


def _in_projection(x, mod, g, w_main, w_vf_t, ssm_w, attn_w):
    bsz, s, d = x.shape
    tm = min(TOKEN_TILE, s)
    pairs = attn_w // LANES
    heads = w_vf_t.shape[0] - attn_w
    kern = functools.partial(_in_kernel, ssm_w=ssm_w, attn_w=attn_w)
    qk_shape = jax.ShapeDtypeStruct((bsz, pairs, s, LANES), BF16)
    qk_spec = pl.BlockSpec((1, pairs, tm, LANES), lambda b, i: (b, 0, i, 0))
    return pl.pallas_call(
        kern,
        grid=(bsz, s // tm),
        in_specs=[pl.BlockSpec((1, tm, d), lambda b, i: (b, i, 0)),
                  pl.BlockSpec(mod.shape, lambda b, i: (0, 0, 0)),
                  pl.BlockSpec((1, d), lambda b, i: (0, 0)),
                  pl.BlockSpec(w_main.shape, lambda b, i: (0, 0)),
                  pl.BlockSpec(w_vf_t.shape, lambda b, i: (0, 0))],
        out_specs=[pl.BlockSpec((1, tm, ssm_w), lambda b, i: (b, i, 0)),
                   pl.BlockSpec((1, pairs, 2, tm, LANES), lambda b, i: (b, 0, 0, i, 0)),
                   qk_spec,
                   pl.BlockSpec((1, pairs, LANES, tm), lambda b, i: (b, 0, 0, i)),
                   pl.BlockSpec((1, heads, tm), lambda b, i: (b, 0, i))],
        out_shape=[jax.ShapeDtypeStruct((bsz, s, ssm_w), F32),
                   jax.ShapeDtypeStruct((bsz, pairs, 2, s, LANES), BF16),
                   qk_shape,
                   jax.ShapeDtypeStruct((bsz, pairs, LANES, s), BF16),
                   jax.ShapeDtypeStruct((bsz, heads, s), F32)],
        compiler_params=_params("arbitrary", "arbitrary"),
        name="in_projection",
    )(x, mod, g.reshape(1, d), w_main, w_vf_t)


def _cum_kernel(f_ref, bf_ref, o_ref):
    x = jax.nn.log_sigmoid(f_ref[0] + bf_ref[...])
    s = x.shape[1]
    lane = lax.broadcasted_iota(jnp.int32, x.shape, 1)
    shift = 1
    while shift < s:
        x = x + jnp.where(lane >= shift, pltpu.roll(x, shift, axis=1), 0.0)
        shift *= 2
    x = x * LOG2E
    hi = x.astype(BF16).astype(F32)
    mid = (x - hi).astype(BF16).astype(F32)
    lo = (x - hi - mid).astype(BF16).astype(F32)
    heads = x.shape[0]
    rows = jnp.concatenate([hi, mid, lo, jnp.zeros((LANES - 3 * heads, s), F32)], axis=0)
    o_ref[0] = rows.T.astype(BF16)


def _forget_cumsum(f_t, b_f):
    bsz, heads, s = f_t.shape
    return pl.pallas_call(
        _cum_kernel,
        grid=(bsz,),
        in_specs=[pl.BlockSpec((1, heads, s), lambda b: (b, 0, 0)),
                  pl.BlockSpec((heads, 1), lambda b: (0, 0))],
        out_specs=pl.BlockSpec((1, s, LANES), lambda b: (b, 0, 0)),
        out_shape=jax.ShapeDtypeStruct((bsz, s, LANES), BF16),
        compiler_params=_params("arbitrary"),
        name="forget_cumsum",
    )(f_t, b_f.reshape(heads, 1))


def _ssm_kernel(u_ref, bm_ref, are_ref, aim_ref, cm_ref, dsk_ref, wg_ref, bg_ref, o_ref,
                buf, st, *, steps, n_slabs):
    @pl.when(pl.program_id(0) == 0)
    def _():
        st[...] = jnp.zeros_like(st)

    bsz, _, width = u_ref.shape
    rows = steps * bsz
    u = jnp.swapaxes(u_ref[...], 0, 1).reshape(rows, width)
    ub = u.astype(BF16)
    slabs_per_block = LANES // (2 * SSM_GROUP_CH)
    for j in range(n_slabs):
        m = j // slabs_per_block
        buf[j] = jnp.dot(ub[:, m * LANES:(m + 1) * LANES], bm_ref[j],
                         preferred_element_type=F32)

    for first in range(0, n_slabs, SCAN_GROUP):
        slabs = list(range(first, first + SCAN_GROUP))
        coefs = [(are_ref[j], aim_ref[j]) for j in slabs]
        init = tuple((st[j, :, :LANES], st[j, :, LANES:]) for j in slabs)

        def body(t, carry, slabs=slabs, coefs=coefs):
            row = t * SUBLANES
            new = []
            for (re, im), (are, aim), j in zip(carry, coefs, slabs):
                bre = buf[j, pl.ds(row, SUBLANES), :LANES]
                bim = buf[j, pl.ds(row, SUBLANES), LANES:]
                nre = are * re - aim * im + bre
                nim = are * im + aim * re + bim
                buf[j, pl.ds(row, SUBLANES), :LANES] = nre
                buf[j, pl.ds(row, SUBLANES), LANES:] = nim
                new.append((nre, nim))
            return tuple(new)

        fin = init
        for t in range(steps):
            fin = body(t, fin)
        for (re, im), j in zip(fin, slabs):
            st[j, :, :LANES] = re
            st[j, :, LANES:] = im

    ys = []
    for m in range(n_slabs // slabs_per_block):
        acc = None
        for j in range(m * slabs_per_block, (m + 1) * slabs_per_block):
            part = jnp.dot(buf[j].astype(BF16), cm_ref[j], preferred_element_type=F32)
            acc = part if acc is None else acc + part
        ys.append(acc)
    y = jnp.concatenate(ys, axis=1) + dsk_ref[...] * u
    z = jax.nn.gelu(y, approximate=True)
    gate = jax.nn.sigmoid(
        jnp.dot(z.astype(BF16), wg_ref[...], preferred_element_type=F32) + bg_ref[...])
    out = (z * gate).reshape(steps, bsz, width)
    o_ref[...] = jnp.swapaxes(out, 0, 1).astype(BF16)


def _ssm_branch(u, bmat, a_re, a_im, cmat, d_skip, w_glu, b_glu):
    bsz, s, width = u.shape
    steps = min(SCAN_STEPS, s)
    rows = steps * bsz
    n_slabs = bmat.shape[0]
    kern = functools.partial(_ssm_kernel, steps=steps, n_slabs=n_slabs)
    full = lambda a: pl.BlockSpec(a.shape, lambda i: (0,) * a.ndim)
    d_skip = d_skip.reshape(1, width)
    b_glu = b_glu.reshape(1, width)
    return pl.pallas_call(
        kern,
        grid=(s // steps,),
        in_specs=[pl.BlockSpec((bsz, steps, width), lambda i: (0, i, 0)),
                  full(bmat), full(a_re), full(a_im), full(cmat),
                  full(d_skip), full(w_glu), full(b_glu)],
        out_specs=pl.BlockSpec((bsz, steps, width), lambda i: (0, i, 0)),
        out_shape=jax.ShapeDtypeStruct((bsz, s, width), BF16),
        scratch_shapes=[pltpu.VMEM((n_slabs, rows, SLAB), F32),
                        pltpu.VMEM((n_slabs, bsz, SLAB), F32)],
        compiler_params=_params("arbitrary"),
        name="s5_branch",
    )(u, bmat, a_re, a_im, cmat, d_skip, w_glu, b_glu)


def _s5_matrices(lam_re, lam_im, log_dt, b_re, b_im, c_re, c_im, bsz):
    g, p = lam_re.shape
    h = b_re.shape[-1]
    lr = jnp.minimum(lam_re, -EIG_CLIP)
    li = lam_im
    dt = jnp.exp(log_dt)[:, None]
    mag = jnp.exp(lr * dt)
    a_re = mag * jnp.cos(li * dt)
    a_im = mag * jnp.sin(li * dt)
    den = lr * lr + li * li
    f_re = ((a_re - 1.0) * lr + a_im * li) / den
    f_im = (a_im * lr - (a_re - 1.0) * li) / den
    bb_re = f_re[..., None] * b_re - f_im[..., None] * b_im
    bb_im = f_re[..., None] * b_im + f_im[..., None] * b_re
    n_slabs = g // 2
    per_block = LANES // (2 * h)
    n_blocks = n_slabs // per_block
    eye_s = jnp.eye(per_block, dtype=F32)
    eye_g = jnp.eye(2, dtype=F32)

    def expand(w_hp, spec):
        w = w_hp.reshape(n_blocks, per_block, 2, h, p)
        return jnp.einsum(spec, w, eye_s, eye_g)

    b_spec = "mjghp,ja,gk->mjaghkp"
    bmat = jnp.concatenate(
        [expand(bb.transpose(0, 2, 1), b_spec).reshape(n_slabs, LANES, LANES)
         for bb in (bb_re, bb_im)], axis=2)
    c_spec = "mjghp,ja,gk->mjkpagh"
    cmat = jnp.concatenate(
        [expand(cc, c_spec).reshape(n_slabs, LANES, LANES) for cc in (c_re, -c_im)], axis=1)
    a_re_s = a_re.reshape(n_slabs, 1, 2 * p)
    a_im_s = a_im.reshape(n_slabs, 1, 2 * p)
    a_re_s = jnp.broadcast_to(a_re_s, (n_slabs, bsz, 2 * p))
    a_im_s = jnp.broadcast_to(a_im_s, (n_slabs, bsz, 2 * p))
    return bmat.astype(BF16), a_re_s, a_im_s, cmat.astype(BF16)


def _attn_kernel(ti_ref, tj_ref, q_ref, k_ref, vt_ref, cum_ref, o_ref,
                 s_buf, mx_buf, p_buf, a_buf, m_sc, l_sc, acc_sc,
                 *, tq, tk, n_diag, kinds, n_heads):
    m_sc[...] = jnp.full_like(m_sc, -jnp.inf)
    l_sc[...] = jnp.zeros_like(l_sc)
    acc_sc[...] = jnp.zeros_like(acc_sc)
    p_buf[1] = jnp.zeros(p_buf.shape[1:], BF16)
    a_buf[1] = jnp.ones(a_buf.shape[1:], F32)

    lane = lax.broadcasted_iota(jnp.int32, (tq, LANES), 1)
    minus_one = []
    for hh in range(2):
        head = 2 * pl.program_id(1) + hh
        mine = (lane < 3 * n_heads) & (lane % n_heads == head)
        minus_one.append(jnp.where(mine, -1.0, 0.0).astype(BF16))

    n_steps = len(kinds)

    def kind_of(n):
        return kinds[min(max(n, 0), n_steps - 1)]

    def tile_of(n):
        i = ti_ref[n]
        return i, pl.multiple_of(i * tq, tq), pl.multiple_of(tj_ref[n] * tk, tk)

    def scores(n, slot, kind):
        masked, width = kind
        _, qs, ks = tile_of(n)
        keys = jnp.concatenate([k_ref[0, 0, pl.ds(ks, width), :],
                                cum_ref[0, pl.ds(ks, width), :]], axis=1)
        for hh in range(2):
            queries = jnp.concatenate([q_ref[0, 0, hh, pl.ds(qs, tq), :], minus_one[hh]], axis=1)
            s = lax.dot_general(keys, queries, (((1,), (1,)), ((), ())),
                                preferred_element_type=F32)
            if masked:
                kpos = ks + lax.broadcasted_iota(jnp.int32, s.shape, 0)
                qpos = qs + lax.broadcasted_iota(jnp.int32, s.shape, 1)
                s = jnp.where(kpos <= qpos, s, -jnp.inf)
            s_buf[slot, hh, :width, :] = s
            mx_buf[slot, hh] = jnp.max(s, axis=0, keepdims=True)

    def softmax(n, slot, kind):
        _, width = kind
        i, _, _ = tile_of(n)
        for hh in range(2):
            m_old = m_sc[i, hh]
            m_new = jnp.maximum(m_old, mx_buf[slot, hh])
            m_sc[i, hh] = m_new
            alpha = jnp.exp2(m_old - m_new)
            pexp = jnp.exp2(s_buf[slot, hh, :width, :] - m_new)
            a_buf[slot, hh] = alpha
            p_buf[slot, hh, :width, :] = pexp.astype(BF16)

    def values(n, slot, kind):
        _, width = kind
        i, _, ks = tile_of(n)
        vt = vt_ref[0, 0, :, pl.ds(ks, width)]
        ones = jnp.ones((2 * SUBLANES, width), BF16)
        for hh in range(2):
            rows = slice(hh * HEAD_DIM, (hh + 1) * HEAD_DIM)
            lhs = jnp.concatenate([vt[rows, :], ones], axis=0)
            res = jnp.dot(lhs, p_buf[slot, hh, :width, :], preferred_element_type=F32)
            alpha = a_buf[slot, hh]
            acc_sc[i, rows, :] = alpha * acc_sc[i, rows, :] + res[:HEAD_DIM, :]
            l_sc[i, hh] = alpha * l_sc[i, hh] + res[HEAD_DIM:HEAD_DIM + 1, :]

    def step_pair(window):
        k_prev, k_even, k_odd, k_next = window

        def body(t, carry):
            n = 2 * t
            prev = max(n - 1, 0) if isinstance(n, int) else jnp.maximum(n - 1, 0)
            scores(n + 1, 1, k_odd)
            softmax(n, 0, k_even)
            values(prev, 1, k_prev)
            scores(n + 2, 0, k_next)
            softmax(n + 1, 1, k_odd)
            values(n, 0, k_even)
            return carry
        return body

    windows = [tuple(kind_of(2 * t + d) for d in (-1, 0, 1, 2)) for t in range(n_steps // 2)]
    scores(0, 0, kind_of(0))
    t0 = 0
    while t0 < len(windows):
        t1 = t0
        while t1 < len(windows) and windows[t1] == windows[t0]:
            t1 += 1
        if t1 - t0 > 1:
            lax.fori_loop(t0, t1, step_pair(windows[t0]), 0)
        else:
            step_pair(windows[t0])(t0, 0)
        t0 = t1
    values(n_steps - 1, 1, kind_of(n_steps - 1))

    def finalize(i, carry):
        out_t = jnp.concatenate([acc_sc[i, :HEAD_DIM, :] / l_sc[i, 0],
                                 acc_sc[i, HEAD_DIM:, :] / l_sc[i, 1]], axis=0)
        o_ref[0, pl.ds(pl.multiple_of(i * tq, tq), tq), :] = out_t.T.astype(BF16)
        return carry

    lax.fori_loop(0, n_diag, finalize, 0)


def _attention_steps(s, tile, wide):
    nq = s // tile
    per_wide = wide // tile
    diag = [(i, i) for i in range(nq)]
    narrow = [(i, j) for i in range(nq) for j in range(i // per_wide * per_wide, i)]
    broad = [(i, w * per_wide) for i in range(nq) for w in range(i // per_wide)]
    groups = [(diag, (True, tile)), (narrow, (False, tile)), (broad, (False, wide))]
    assert all(len(g) % 2 == 0 for g, _ in groups), "the pipeline advances two steps at a time"
    steps = [st for g, _ in groups for st in g]
    kinds = tuple(kind for g, kind in groups for _ in g)
    padded = steps + [steps[-1]] * 2
    ti = jnp.asarray([i for i, _ in padded], jnp.int32)
    tj = jnp.asarray([j for _, j in padded], jnp.int32)
    return ti, tj, nq, kinds


def _attention(q, k, v_t, cum_parts):
    bsz, pairs, _, s, _ = q.shape
    tq = min(ATTN_Q_TILE, s)
    wide = min(ATTN_K_WIDE, s)
    ti, tj, nq, kinds = _attention_steps(s, tq, wide)
    tk = max(width for _, width in kinds)
    kern = functools.partial(_attn_kernel, tq=tq, tk=tq, n_diag=nq, kinds=kinds,
                             n_heads=2 * pairs)
    grid_spec = pltpu.PrefetchScalarGridSpec(
        num_scalar_prefetch=2,
        grid=(bsz, pairs),
        in_specs=[pl.BlockSpec((1, 1, 2, s, LANES), lambda b, p, ti, tj: (b, p, 0, 0, 0)),
                  pl.BlockSpec((1, 1, s, LANES), lambda b, p, ti, tj: (b, p, 0, 0)),
                  pl.BlockSpec((1, 1, LANES, s), lambda b, p, ti, tj: (b, p, 0, 0)),
                  pl.BlockSpec((1, s, LANES), lambda b, p, ti, tj: (b, 0, 0))],
        out_specs=pl.BlockSpec((1, s, LANES), lambda b, p, ti, tj: (b, 0, p)),
        scratch_shapes=[pltpu.VMEM((2, 2, tk, tq), F32),
                        pltpu.VMEM((2, 2, 1, tq), F32),
                        pltpu.VMEM((2, 2, tk, tq), BF16),
                        pltpu.VMEM((2, 2, 1, tq), F32),
                        pltpu.VMEM((nq, 2, 1, tq), F32),
                        pltpu.VMEM((nq, 2, 1, tq), F32),
                        pltpu.VMEM((nq, LANES, tq), F32)])
    return pl.pallas_call(
        kern,
        grid_spec=grid_spec,
        out_shape=jax.ShapeDtypeStruct((bsz, s, pairs * LANES), BF16),
        compiler_params=_params("arbitrary", "arbitrary"),
        name="forgetting_attention",
    )(ti, tj, q, k, v_t, cum_parts)


def _merge_kernel(x_ref, mod_ref, gpre_ref, gpost_ref, wgate_ref, ys_ref, ya_ref,
                  wpa_ref, wpb_ref, wo_ref, o_ref, *, sub):
    b = pl.program_id(0)
    d = x_ref.shape[-1]
    for r0 in range(0, x_ref.shape[1], sub):
        rows = slice(r0, r0 + sub)
        x = x_ref[0, rows, :]
        h = _rmsnorm(x, gpre_ref[...]) * (1.0 + _mod_row(mod_ref, 1, b)) + _mod_row(mod_ref, 0, b)
        gates = jnp.dot(h.astype(BF16), wgate_ref[...], preferred_element_type=F32)
        pa = jnp.dot(ys_ref[0, rows, :], wpa_ref[...], preferred_element_type=F32)
        pb = jnp.dot(ya_ref[0, rows, :], wpb_ref[...], preferred_element_type=F32)
        merged = jax.nn.sigmoid(gates[:, :d]) * pa + jax.nn.sigmoid(gates[:, d:]) * pb
        y = jnp.dot(merged.astype(BF16), wo_ref[...], preferred_element_type=F32)
        o_ref[0, rows, :] = x + _mod_row(mod_ref, 2, b) * _rmsnorm(y, gpost_ref[...])


def _resident(a, layer=None):
    if layer is None:
        return pl.BlockSpec(a.shape, lambda b, i: (0,) * a.ndim, pipeline_mode=pl.Buffered(1))
    return pl.BlockSpec((None,) + a.shape[1:], lambda b, i: (layer,) + (0,) * (a.ndim - 1),
                        pipeline_mode=pl.Buffered(1))


def _merge(x, mod, g_pre, g_post, w_gate, y_ssm, y_att, w_pa, w_pb, w_o, layer):
    bsz, s, d = x.shape
    tm = min(2 * TOKEN_TILE, s)
    width = y_att.shape[-1]
    full = lambda a: pl.BlockSpec(a.shape, lambda b, i: (0,) * a.ndim)
    g_pre = g_pre.reshape(1, d)
    g_post = g_post.reshape(1, d)
    return pl.pallas_call(
        functools.partial(_merge_kernel, sub=min(TOKEN_TILE, tm)),
        grid=(bsz, s // tm),
        in_specs=[pl.BlockSpec((1, tm, d), lambda b, i: (b, i, 0)),
                  full(mod), full(g_pre), full(g_post), _resident(w_gate),
                  pl.BlockSpec((1, tm, width), lambda b, i: (b, i, 0)),
                  pl.BlockSpec((1, tm, width), lambda b, i: (b, i, 0)),
                  _resident(w_pa, layer), _resident(w_pb, layer), _resident(w_o, layer)],
        out_specs=pl.BlockSpec((1, tm, d), lambda b, i: (b, i, 0)),
        out_shape=jax.ShapeDtypeStruct(x.shape, F32),
        compiler_params=_params("arbitrary", "arbitrary"),
        name="gated_merge",
    )(x, mod, g_pre, g_post, w_gate, y_ssm, y_att, w_pa, w_pb, w_o)


def _ffn_chunks(d_ff):
    mxu = 2 * LANES
    n_tiles = d_ff // mxu
    first = (n_tiles + 1) // 2 * mxu
    return ((0, first), (first, d_ff)) if first < d_ff else ((0, d_ff),)


def _ffn_kernel(x_ref, mod_ref, gpre_ref, gpost_ref, wg_ref, wu_ref, wd_ref, o_ref, *, chunks, sub):
    b = pl.program_id(0)
    for r0 in range(0, x_ref.shape[1], sub):
        x = x_ref[0, r0:r0 + sub, :]
        h = _rmsnorm(x, gpre_ref[...]) * (1.0 + _mod_row(mod_ref, 4, b)) + _mod_row(mod_ref, 3, b)
        hb = h.astype(BF16)
        acc = None
        for c0, c1 in chunks:
            g = jnp.dot(hb, wg_ref[:, c0:c1], preferred_element_type=F32)
            up = jnp.dot(hb, wu_ref[:, c0:c1], preferred_element_type=F32)
            a = (g * jax.nn.sigmoid(g) * up).astype(BF16)
            part = jnp.dot(a, wd_ref[c0:c1, :], preferred_element_type=F32)
            acc = part if acc is None else acc + part
        o_ref[0, r0:r0 + sub, :] = x + _mod_row(mod_ref, 5, b) * _rmsnorm(acc, gpost_ref[...])


def _ffn(x, mod, g_pre, g_post, w_gate, w_up, w_down, layer):
    bsz, s, d = x.shape
    tm = min(2 * TOKEN_TILE, s)
    kern = functools.partial(_ffn_kernel, chunks=_ffn_chunks(w_gate.shape[-1]),
                             sub=min(TOKEN_TILE, tm))
    full = lambda a: pl.BlockSpec(a.shape, lambda b, i: (0,) * a.ndim)
    g_pre = g_pre.reshape(1, d)
    g_post = g_post.reshape(1, d)
    return pl.pallas_call(
        kern,
        grid=(bsz, s // tm),
        in_specs=[pl.BlockSpec((1, tm, d), lambda b, i: (b, i, 0)),
                  full(mod), full(g_pre), full(g_post),
                  _resident(w_gate, layer), _resident(w_up, layer), _resident(w_down, layer)],
        out_specs=pl.BlockSpec((1, tm, d), lambda b, i: (b, i, 0)),
        out_shape=jax.ShapeDtypeStruct(x.shape, F32),
        compiler_params=_params("arbitrary", "arbitrary"),
        name="swiglu_ffn",
    )(x, mod, g_pre, g_post, w_gate, w_up, w_down)


def kernel(x, c, w_ada, b_ada, g_pre_mix, g_post_mix, g_pre_ffn, g_post_ffn, w_in, lam_re, lam_im, log_dt, b_re, b_im, c_re, c_im, d_skip, w_glu, b_glu, b_f, w_pa, w_pb, w_o, w_ffn_gate, w_ffn_up, w_ffn_down):
    bsz, s, d = x.shape
    depth = w_in.shape[0]
    ssm_w = d_skip.shape[1]
    heads = b_f.shape[1]
    attn_w = heads * HEAD_DIM
    pairs = attn_w // LANES
    main_w = ssm_w + 3 * attn_w
    assert bsz == SUBLANES, "the S5 scan keeps the batch on the sublane axis"

    mod_all = _modulation(c, w_ada, b_ada)
    w_in_b = w_in.astype(BF16)
    w_pa_b, w_pb_b, w_o_b = w_pa.astype(BF16), w_pb.astype(BF16), w_o.astype(BF16)
    w_ffn = (w_ffn_gate.astype(BF16), w_ffn_up.astype(BF16), w_ffn_down.astype(BF16))
    for l in range(depth):
        mod = mod_all[l]
        uqk_w = ssm_w + 2 * attn_w
        w_main = w_in_b[l, :, :uqk_w]
        w_vf_t = w_in_b[l, :, uqk_w:main_w + heads].T
        w_gate = w_in_b[l, :, main_w + heads:]

        u, q, k, v_t, f_t = _in_projection(x, mod, g_pre_mix[l], w_main, w_vf_t, ssm_w, attn_w)

        cum_parts = _forget_cumsum(f_t, b_f[l])
        y_att = _attention(q, k, v_t, cum_parts)

        bmat, a_re, a_im, cmat = _s5_matrices(lam_re[l], lam_im[l], log_dt[l], b_re[l], b_im[l],
                                              c_re[l], c_im[l], bsz)
        y_ssm = _ssm_branch(u, bmat, a_re, a_im, cmat,
                            d_skip[l], w_glu[l].astype(BF16), b_glu[l])

        x = _merge(x, mod, g_pre_mix[l], g_post_mix[l], w_gate, y_ssm, y_att,
                   w_pa_b, w_pb_b, w_o_b, l)
        x = _ffn(x, mod, g_pre_ffn[l], g_post_ffn[l], *w_ffn, l)
    return x
```
